```python
import math
import jax, jax.numpy as jnp
from jax import lax
import numpy as np

D_MODEL = 1024
BATCH = 8
SEQ = 4096
DEPTH = 2

CTX_LEN = 256
GRID_W = 64
EPS = 1e-6

NA_HEADS = 8
NA_HEAD_DIM = 64
NA_W = NA_HEADS * NA_HEAD_DIM
NA_WIN_ROWS = 8
NA_WIN_COLS = 16
NA_QBLK_COLS = 16
NA_KBLK_COLS = 32

SSM_HEADS = 16
SSM_HEAD_DIM = 64
SSM_D_INNER = SSM_HEADS * SSM_HEAD_DIM
SSM_GROUPS = 2
SSM_D_STATE = 64
SSM_BC = SSM_GROUPS * SSM_D_STATE
SSM_XBC = SSM_D_INNER + 2 * SSM_BC
SSM_CONV = 3
SSM_CHUNK = 128

GQA_HEADS = 8
GQA_KV_HEADS = 2
GQA_HEAD_DIM = 64
GQA_Q = GQA_HEADS * GQA_HEAD_DIM
GQA_KV = GQA_KV_HEADS * GQA_HEAD_DIM
ROPE_THETA = 10000.0
Q_BLOCK = 128

FFN_HIDDEN = 2816
FFN_CONV = 3

N_BRANCHES = 3
IN_SIZES = (3 * NA_W, SSM_D_INNER, SSM_XBC, 2 * SSM_HEADS, GQA_Q, GQA_KV, GQA_KV, N_BRANCHES * D_MODEL)
IN_SPLIT_IDX = tuple(int(i) for i in np.cumsum(IN_SIZES)[:-1])
D_IN_PROJ = sum(IN_SIZES)

kernel_name = 'hybrid_na_ssd_gqa_prefix_dit'

F32 = jnp.float32


def rms_norm(x, w):
    xf = x.astype(F32)
    y = xf * lax.rsqrt(jnp.mean(xf * xf, axis=-1, keepdims=True) + EPS)
    return (y * w.astype(F32)).astype(x.dtype)


def modulate(h, shift, scale):
    return h * (1 + scale) + shift


def heads(t, n_heads):
    return t.reshape(t.shape[0], t.shape[1], n_heads, -1)


def depthwise_conv(x, w, b):
    k = w.shape[0]
    pad = k // 2
    y = lax.conv_general_dilated(x, w[:, None, :], window_strides=(1,), padding=[(pad, pad)],
                                 dimension_numbers=('NWC', 'WIO', 'NWC'), feature_group_count=x.shape[-1])
    return y + b


def axial_rope_tables(n_tok):
    n_freq = GQA_HEAD_DIM // 4
    inv_freq = ROPE_THETA ** (-jnp.arange(n_freq, dtype=F32) / n_freq)
    t = jnp.arange(n_tok, dtype=jnp.int32)
    row = (t // GRID_W).astype(F32)
    col = (t % GRID_W).astype(F32)
    ang = jnp.concatenate([row[:, None] * inv_freq, col[:, None] * inv_freq], axis=-1)
    return jnp.cos(ang), jnp.sin(ang)


def apply_rope(x, cos, sin):
    xf = x.astype(F32)
    x1, x2 = xf[..., 0::2], xf[..., 1::2]
    cs, sn = cos[:, None, :], sin[:, None, :]
    out = jnp.stack([x1 * cs - x2 * sn, x1 * sn + x2 * cs], axis=-1).reshape(x.shape)
    return out.astype(x.dtype)


def context_attention(q, k, v):
    bsz, n, hq, hd = q.shape
    hkv = k.shape[2]
    qg = (q * hd ** -0.5).reshape(bsz, n, hkv, hq // hkv, hd)
    s = jnp.einsum('bqkgd,bskd->bkgqs', qg, k).astype(F32)
    p = jax.nn.softmax(s, axis=-1).astype(v.dtype)
    return jnp.einsum('bkgqs,bskd->bqkgd', p, v).reshape(bsz, n, hq * hd)


def neighbourhood_attention(q, k, v, k_ctx, v_ctx, rel_bias, n_rows):
    bsz, n_tok, nh, hd = q.shape
    kr = min(NA_WIN_ROWS, n_rows)
    n_cb = GRID_W // NA_QBLK_COLS
    qcol = np.arange(GRID_W).reshape(n_cb, NA_QBLK_COLS)
    band0 = np.clip(qcol[:, 0] - NA_WIN_COLS // 2, 0, GRID_W - NA_KBLK_COLS)
    kcol = band0[:, None] + np.arange(NA_KBLK_COLS)
    win0 = np.clip(qcol - NA_WIN_COLS // 2, 0, GRID_W - NA_WIN_COLS)
    kc3 = kcol[:, None, :]
    col_ok = jnp.asarray((kc3 >= win0[..., None]) & (kc3 < win0[..., None] + NA_WIN_COLS))
    dcol = np.clip(kc3 - qcol[..., None] + NA_WIN_COLS - 1, 0, 2 * NA_WIN_COLS - 2)
    qg = (q * hd ** -0.5).reshape(bsz, n_rows, n_cb, NA_QBLK_COLS, nh, hd)
    kg = k.reshape(bsz, n_rows, GRID_W, nh, hd)
    vg = v.reshape(bsz, n_rows, GRID_W, nh, hd)
    n_win = kr * NA_KBLK_COLS

    def row_block(r):
        rs = jnp.clip(r - kr // 2, 0, n_rows - kr)
        kb = lax.dynamic_slice_in_dim(kg, rs, kr, axis=1)[:, :, kcol]
        vb = lax.dynamic_slice_in_dim(vg, rs, kr, axis=1)[:, :, kcol]
        qr = lax.dynamic_index_in_dim(qg, r, axis=1, keepdims=False)
        s_win = jnp.einsum('bjqhd,bijkhd->bhjqik', qr, kb).astype(F32)
        drow = rs + jnp.arange(kr) - r + NA_WIN_ROWS - 1
        bias = rel_bias[:, drow][:, :, dcol].transpose(0, 2, 3, 1, 4).astype(F32)
        s_win = jnp.where(col_ok[None, None, :, :, None, :], s_win + bias[None], -jnp.inf)
        s_ctx = jnp.einsum('bjqhd,bshd->bhjqs', qr, k_ctx).astype(F32)
        s_all = jnp.concatenate([s_win.reshape(bsz, nh, n_cb, NA_QBLK_COLS, n_win), s_ctx], axis=-1)
        p = jax.nn.softmax(s_all, axis=-1).astype(v.dtype)
        p_win = p[..., :n_win].reshape(bsz, nh, n_cb, NA_QBLK_COLS, kr, NA_KBLK_COLS)
        return (jnp.einsum('bhjqik,bijkhd->bjqhd', p_win, vb)
                + jnp.einsum('bhjqs,bshd->bjqhd', p[..., n_win:], v_ctx))

    o = lax.map(row_block, jnp.arange(n_rows))
    return o.transpose(1, 0, 2, 3, 4, 5).reshape(bsz, n_tok, nh * hd)


def gqa_latent(q, k, v, k_ctx, v_ctx):
    bsz, n_tok, hq, hd = q.shape
    hkv = k.shape[2]
    k_all = jnp.concatenate([k_ctx, k], axis=1)
    v_all = jnp.concatenate([v_ctx, v], axis=1)
    qb = (q * hd ** -0.5).reshape(bsz, n_tok // Q_BLOCK, Q_BLOCK, hkv, hq // hkv, hd).transpose(1, 0, 2, 3, 4, 5)

    def block(qi):
        s = jnp.einsum('bqkgd,bskd->bkgqs', qi, k_all).astype(F32)
        p = jax.nn.softmax(s, axis=-1).astype(v_all.dtype)
        return jnp.einsum('bkgqs,bskd->bqkgd', p, v_all)

    o = lax.map(block, qb)
    return o.transpose(1, 0, 2, 3, 4, 5).reshape(bsz, n_tok, hq * hd)


def ssd_scan(x, dt, a, b, c, h0):
    bsz, n, nh, hd = x.shape
    ng, ds = b.shape[2], b.shape[3]
    rep = nh // ng
    tl = SSM_CHUNK
    nc = n // tl
    xs = (x.astype(F32) * dt[..., None]).reshape(bsz, nc, tl, ng, rep, hd)
    bs = b.astype(F32).reshape(bsz, nc, tl, ng, ds)
    cs = c.astype(F32).reshape(bsz, nc, tl, ng, ds)
    da = (dt * a).reshape(bsz, nc, tl, ng, rep).transpose(0, 3, 4, 1, 2)
    da_cs = jnp.cumsum(da, axis=-1)
    tri = jnp.tril(jnp.ones((tl, tl), dtype=bool))
    decay_in = jnp.exp(jnp.where(tri, da_cs[..., :, None] - da_cs[..., None, :], -jnp.inf))
    cb = jnp.einsum('bctgn,bcsgn->bgcts', cs, bs)
    y_diag = jnp.einsum('bgrcts,bcsgrp->bctgrp', cb[:, :, None] * decay_in, xs)
    decay_to_end = jnp.exp(da_cs[..., -1:] - da_cs)
    chunk_states = jnp.einsum('bcsgn,bgrcs,bcsgrp->bcgrpn', bs, decay_to_end, xs)
    states = jnp.concatenate([h0.reshape(bsz, 1, ng, rep, hd, ds), chunk_states], axis=1)
    chunk_cs = jnp.cumsum(jnp.pad(da_cs[..., -1], ((0, 0), (0, 0), (0, 0), (1, 0))), axis=-1)
    tri_c = jnp.tril(jnp.ones((nc + 1, nc + 1), dtype=bool))
    decay_chunk = jnp.exp(jnp.where(tri_c, chunk_cs[..., :, None] - chunk_cs[..., None, :], -jnp.inf))
    states = jnp.einsum('bgrzc,bcgrpn->bzgrpn', decay_chunk, states)
    y_off = jnp.einsum('bctgn,bcgrpn,bgrct->bctgrp', cs, states[:, :-1], jnp.exp(da_cs))
    y = (y_diag + y_off).reshape(bsz, n, nh, hd)
    return y, states[:, -1].reshape(bsz, nh, hd, ds)


def ssd_mixer(z_lat, xbc_lat, dt_lat, z_ctx, xbc_ctx, dt_ctx, conv_w, conv_b, a_log, dt_bias, d_skip, norm_w, need_ctx):
    a = -jnp.exp(a_log.astype(F32))

    def prep(xbc, dt_raw):
        bsz, n = xbc.shape[:2]
        xbc = jax.nn.silu(depthwise_conv(xbc, conv_w, conv_b))
        xs, bs, cs = jnp.split(xbc, [SSM_D_INNER, SSM_D_INNER + SSM_BC], axis=-1)
        dt = jax.nn.softplus(dt_raw.astype(F32).reshape(bsz, n, 2, SSM_HEADS) + dt_bias.astype(F32))
        return (xs.reshape(bsz, n, SSM_HEADS, SSM_HEAD_DIM), bs.reshape(bsz, n, SSM_GROUPS, SSM_D_STATE),
                cs.reshape(bsz, n, SSM_GROUPS, SSM_D_STATE), dt)

    x_l, b_l, c_l, dt_l = prep(xbc_lat, dt_lat)
    x_c, b_c, c_c, dt_c = prep(xbc_ctx, dt_ctx)
    h0 = jnp.zeros((x_c.shape[0], SSM_HEADS, SSM_HEAD_DIM, SSM_D_STATE), F32)
    rev = lambda t: jnp.flip(t, axis=1)
    yf_c, hf_c = ssd_scan(x_c, dt_c[:, :, 0], a[0], b_c, c_c, h0)
    yf_l, _ = ssd_scan(x_l, dt_l[:, :, 0], a[0], b_l, c_l, hf_c)
    yb_c, hb_c = ssd_scan(rev(x_c), rev(dt_c[:, :, 1]), a[1], rev(b_c), rev(c_c), h0)
    yb_l, _ = ssd_scan(rev(x_l), rev(dt_l[:, :, 1]), a[1], rev(b_l), rev(c_l), hb_c)

    def finish(yf, yb, xs, z):
        y = (yf + rev(yb) + d_skip.astype(F32)[:, None] * xs.astype(F32)).reshape(z.shape)
        return rms_norm(y * jax.nn.silu(z.astype(F32)), norm_w).astype(z.dtype)

    out_l = finish(yf_l, yb_l, x_l, z_lat)
    out_c = finish(yf_c, yb_c, x_c, z_ctx) if need_ctx else None
    return out_l, out_c


def token_mixing(p_lat, p_ctx, n_rows, cos, sin, na_rel_bias, ssm_conv_w, ssm_conv_b, ssm_a_log, ssm_dt_bias,
                 ssm_d, ssm_norm_w, q_norm_w, k_norm_w, w_out_na, w_out_ssm, w_out_gqa, w_o, need_ctx):
    na_l, z_l, xbc_l, dt_l, gq_l, gk_l, gv_l, gate_l = jnp.split(p_lat, IN_SPLIT_IDX, axis=-1)
    na_c, z_c, xbc_c, dt_c, gq_c, gk_c, gv_c, gate_c = jnp.split(p_ctx, IN_SPLIT_IDX, axis=-1)
    qa_l, ka_l, va_l = (heads(t, NA_HEADS) for t in jnp.split(na_l, 3, axis=-1))
    qa_c, ka_c, va_c = (heads(t, NA_HEADS) for t in jnp.split(na_c, 3, axis=-1))
    o_na_l = neighbourhood_attention(qa_l, ka_l, va_l, ka_c, va_c, na_rel_bias, n_rows)
    o_ssm_l, o_ssm_c = ssd_mixer(z_l, xbc_l, dt_l, z_c, xbc_c, dt_c, ssm_conv_w, ssm_conv_b, ssm_a_log,
                                 ssm_dt_bias, ssm_d, ssm_norm_w, need_ctx)
    q_l = apply_rope(rms_norm(heads(gq_l, GQA_HEADS), q_norm_w), cos, sin)
    k_l = apply_rope(rms_norm(heads(gk_l, GQA_KV_HEADS), k_norm_w), cos, sin)
    q_c = rms_norm(heads(gq_c, GQA_HEADS), q_norm_w)
    k_c = rms_norm(heads(gk_c, GQA_KV_HEADS), k_norm_w)
    v_l, v_c = heads(gv_l, GQA_KV_HEADS), heads(gv_c, GQA_KV_HEADS)
    o_gqa_l = gqa_latent(q_l, k_l, v_l, k_c, v_c)

    def merge(gate, o_na, o_ssm, o_gqa):
        g_na, g_ssm, g_gqa = jnp.split(jax.nn.sigmoid(gate), N_BRANCHES, axis=-1)
        return (g_na * (o_na @ w_out_na) + g_ssm * (o_ssm @ w_out_ssm) + g_gqa * (o_gqa @ w_out_gqa)) @ w_o

    mix_l = merge(gate_l, o_na_l, o_ssm_l, o_gqa_l)
    mix_c = None
    if need_ctx:
        mix_c = merge(gate_c, context_attention(qa_c, ka_c, va_c), o_ssm_c, context_attention(q_c, k_c, v_c))
    return mix_l, mix_c


def conv_ffn(h, w_up, conv_w, conv_b, w_down):
    u = depthwise_conv(h @ w_up, conv_w, conv_b)
    a, b = jnp.split(u, 2, axis=-1)
    return (jax.nn.silu(a) * b) @ w_down


def setup_inputs(seed: int = 0) -> dict:
    key = jax.random.key(seed)
    ks = jax.random.split(key, 32)
    L, D, F = DEPTH, D_MODEL, FFN_HIDDEN

    def normal(k, shape, scale):
        return jax.random.normal(k, shape, F32) * scale

    dt0 = jnp.exp(jax.random.uniform(ks[13], (L, 2, SSM_HEADS), F32, math.log(1e-3), math.log(1e-1)))
    return {
        'x': normal(ks[0], (BATCH, SEQ, D), 1.0),
        'c': normal(ks[1], (BATCH, D), 1.0),
        'ctx': normal(ks[2], (BATCH, CTX_LEN, D), 1.0),
        'c_ctx': normal(ks[3], (D,), 1.0),
        'w_mod': normal(ks[4], (L, D, 6 * D), 0.5 * D ** -0.5),
        'b_mod': normal(ks[5], (L, 6 * D), 0.02),
        'norm1_w': 1.0 + normal(ks[6], (L, D), 0.02),
        'norm2_w': 1.0 + normal(ks[7], (L, D), 0.02),
        'w_in': normal(ks[8], (L, D, D_IN_PROJ), D ** -0.5),
        'na_rel_bias': normal(ks[9], (L, NA_HEADS, 2 * NA_WIN_ROWS - 1, 2 * NA_WIN_COLS - 1), 0.1),
        'ssm_conv_w': normal(ks[10], (L, SSM_CONV, SSM_XBC), SSM_CONV ** -0.5),
        'ssm_conv_b': normal(ks[11], (L, SSM_XBC), 0.02),
        'ssm_a_log': jnp.log(jax.random.uniform(ks[12], (L, 2, SSM_HEADS), F32, 1.0, 16.0)),
        'ssm_dt_bias': dt0 + jnp.log(-jnp.expm1(-dt0)),
        'ssm_d': 1.0 + normal(ks[14], (L, SSM_HEADS), 0.1),
        'ssm_norm_w': 1.0 + normal(ks[15], (L, SSM_D_INNER), 0.02),
        'q_norm_w': 1.0 + normal(ks[16], (L, GQA_HEAD_DIM), 0.02),
        'k_norm_w': 1.0 + normal(ks[17], (L, GQA_HEAD_DIM), 0.02),
        'w_out_na': normal(ks[18], (L, NA_W, D), NA_W ** -0.5),
        'w_out_ssm': normal(ks[19], (L, SSM_D_INNER, D), SSM_D_INNER ** -0.5),
        'w_out_gqa': normal(ks[20], (L, GQA_Q, D), GQA_Q ** -0.5),
        'w_o': normal(ks[21], (L, D, D), D ** -0.5),
        'ffn_w_up': normal(ks[22], (L, D, 2 * F), D ** -0.5),
        'ffn_conv_w': normal(ks[23], (L, FFN_CONV, 2 * F), FFN_CONV ** -0.5),
        'ffn_conv_b': normal(ks[24], (L, 2 * F), 0.02),
        'ffn_w_down': normal(ks[25], (L, F, D), F ** -0.5),
        'final_norm_w': 1.0 + normal(ks[26], (D,), 0.02),
    }


def reference(x, c, ctx, c_ctx, w_mod, b_mod, norm1_w, norm2_w, w_in, na_rel_bias, ssm_conv_w, ssm_conv_b,
              ssm_a_log, ssm_dt_bias, ssm_d, ssm_norm_w, q_norm_w, k_norm_w, w_out_na, w_out_ssm, w_out_gqa,
              w_o, ffn_w_up, ffn_conv_w, ffn_conv_b, ffn_w_down, final_norm_w):
    n_tok = x.shape[1]
    n_rows = n_tok // GRID_W
    cos, sin = axial_rope_tables(n_tok)
    h_lat, h_ctx = x, ctx
    for layer in range(DEPTH):
        need_ctx = layer < DEPTH - 1
        mod_lat = (jax.nn.silu(c) @ w_mod[layer] + b_mod[layer])[:, None, :]
        mod_ctx = (jax.nn.silu(c_ctx) @ w_mod[layer] + b_mod[layer])[None, None, :]
        sh1_l, sc1_l, g1_l, sh2_l, sc2_l, g2_l = jnp.split(mod_lat, 6, axis=-1)
        sh1_c, sc1_c, g1_c, sh2_c, sc2_c, g2_c = jnp.split(mod_ctx, 6, axis=-1)
        p_lat = modulate(rms_norm(h_lat, norm1_w[layer]), sh1_l, sc1_l) @ w_in[layer]
        p_ctx = modulate(rms_norm(h_ctx, norm1_w[layer]), sh1_c, sc1_c) @ w_in[layer]
        mix_l, mix_c = token_mixing(p_lat, p_ctx, n_rows, cos, sin, na_rel_bias[layer], ssm_conv_w[layer],
                                    ssm_conv_b[layer], ssm_a_log[layer], ssm_dt_bias[layer], ssm_d[layer],
                                    ssm_norm_w[layer], q_norm_w[layer], k_norm_w[layer], w_out_na[layer],
                                    w_out_ssm[layer], w_out_gqa[layer], w_o[layer], need_ctx)
        h_lat = h_lat + g1_l * mix_l
        h_lat = h_lat + g2_l * conv_ffn(modulate(rms_norm(h_lat, norm2_w[layer]), sh2_l, sc2_l), ffn_w_up[layer],
                                        ffn_conv_w[layer], ffn_conv_b[layer], ffn_w_down[layer])
        if need_ctx:
            h_ctx = h_ctx + g1_c * mix_c
            h_ctx = h_ctx + g2_c * conv_ffn(modulate(rms_norm(h_ctx, norm2_w[layer]), sh2_c, sc2_c),
                                            ffn_w_up[layer], ffn_conv_w[layer], ffn_conv_b[layer], ffn_w_down[layer])
    return rms_norm(h_lat, final_norm_w)
```

```python
import functools

import numpy as np
import jax
import jax.numpy as jnp
from jax import lax
from jax.experimental import pallas as pl
from jax.experimental.pallas import tpu as pltpu

F32 = jnp.float32
BF16 = jnp.bfloat16

D_MODEL = 1024
GRID_W = 64
EPS = 1e-6
HEAD_DIM = 64
NA_HEADS = 8
NA_W = NA_HEADS * HEAD_DIM
NA_WIN_ROWS = 8
NA_WIN_COLS = 16
SSM_HEADS = 16
SSM_D_INNER = SSM_HEADS * HEAD_DIM
SSM_GROUPS = 2
SSM_D_STATE = 64
SSM_BC = SSM_GROUPS * SSM_D_STATE
SSM_XBC = SSM_D_INNER + 2 * SSM_BC
SSM_CHUNK = 128
GQA_HEADS = 8
GQA_KV_HEADS = 2
GQA_GROUP = GQA_HEADS // GQA_KV_HEADS
GQA_Q = GQA_HEADS * HEAD_DIM
GQA_KV = GQA_KV_HEADS * HEAD_DIM
ROPE_THETA = 10000.0
FFN_HIDDEN = 2816
N_MOD = 6
MOD_ROWS = 16
CTX_MOD_ROW = 8
NEG_BIG = -1e30

LANES = 128
BF16_ROWS = 16
F32_ROWS = 8
VMEM_LIMIT = 56 * 1024 * 1024

SEG_NA = (0, 3 * NA_W)
SEG_Z = (SEG_NA[1], SEG_NA[1] + SSM_D_INNER)
SEG_XBC = (SEG_Z[1], SEG_Z[1] + SSM_XBC)
SEG_GQ = (SEG_XBC[1], SEG_XBC[1] + GQA_Q)
SEG_GK = (SEG_GQ[1], SEG_GQ[1] + GQA_KV)
SEG_GV = (SEG_GK[1], SEG_GK[1] + GQA_KV)
SEG_GATE = (SEG_GV[1], SEG_GV[1] + 3 * D_MODEL)
W_MAIN_COLS = SEG_GATE[1]
DT_COLS = 2 * LANES
DOT_COLS = 512

FFN_TN = 256
FFN_CHUNKS = FFN_HIDDEN // FFN_TN


def _mm(a, b):
    return jnp.dot(a, b, preferred_element_type=F32)


def _mm_nt(a, b):
    return lax.dot_general(a, b, (((1,), (1,)), ((), ())), preferred_element_type=F32)


def _split_bf16(x):
    hi = x.astype(BF16)
    lo = (x - hi.astype(F32)).astype(BF16)
    return hi, lo


def _sigmoid(x):
    return 1.0 / (1.0 + jnp.exp(-x))


def _silu(x):
    return x * _sigmoid(x)


def _softplus(x):
    return jnp.maximum(x, 0.0) + jnp.log1p(jnp.exp(-jnp.abs(x)))


def _rms(x):
    return x * lax.rsqrt(jnp.mean(x * x, axis=-1, keepdims=True) + EPS)


def _resident(shape):
    nd = len(shape)
    return pl.BlockSpec(shape, lambda *_: (0,) * nd, pipeline_mode=pl.Buffered(1))


def _params(sem):
    return pltpu.CompilerParams(dimension_semantics=sem, vmem_limit_bytes=VMEM_LIMIT)


def _mod_kernel(c_ref, w_ref, b_ref, o_ref):
    x_hi, x_lo = _split_bf16(_silu(c_ref[...]))
    w_hi, w_lo = _split_bf16(w_ref[...])
    o_ref[...] = _mm(x_hi, w_hi) + _mm(x_lo, w_hi) + _mm(x_hi, w_lo) + b_ref[...]


def _mod_vectors(cc, w_mod, b_mod):
    n_layers = w_mod.shape[0]
    tn = D_MODEL
    return pl.pallas_call(
        _mod_kernel,
        out_shape=jax.ShapeDtypeStruct((n_layers, MOD_ROWS, N_MOD * D_MODEL), F32),
        grid=(n_layers, N_MOD),
        in_specs=[
            pl.BlockSpec((MOD_ROWS, D_MODEL), lambda l, j: (0, 0)),
            pl.BlockSpec((None, D_MODEL, tn), lambda l, j: (l, 0, j)),
            pl.BlockSpec((None, 1, tn), lambda l, j: (l, 0, j)),
        ],
        out_specs=pl.BlockSpec((None, MOD_ROWS, tn), lambda l, j: (l, 0, j)),
        compiler_params=_params(("arbitrary", "arbitrary")),
        name="mod_vectors",
    )(cc, w_mod, b_mod.reshape(n_layers, 1, N_MOD * D_MODEL))


def _mod_spec(which, row_fn):
    return pl.BlockSpec((None, None, 1, D_MODEL), lambda i: (row_fn(i), which, 0, 0))


def _qk_norm_rope(x, w, cos, sin, seg_ones):
    sq_hi, sq_lo = _split_bf16(x * x)
    ss = _mm(sq_hi, seg_ones) + _mm(sq_lo, seg_ones)
    xn = x * lax.rsqrt(ss * (1.0 / HEAD_DIM) + EPS) * w
    lane = lax.broadcasted_iota(jnp.int32, xn.shape, 1)
    partner = jnp.where((lane & 1) == 0, pltpu.roll(xn, LANES - 1, 1), pltpu.roll(xn, 1, 1))
    return xn * cos + partner * sin


def _inproj_kernel(h_ref, sh_ref, sc_ref, nw_ref, cos_ref, sin_ref, qw_ref, kw_ref, seg_ref,
                   w_ref, wdh_ref, wdl_ref,
                   na_ref, z_ref, xbc_ref, dt_ref, q_ref, k_ref, v_ref, gate_ref):
    y = _rms(h_ref[...]) * nw_ref[...]
    y = y * (1.0 + sc_ref[...]) + sh_ref[...]
    y_hi, y_lo = _split_bf16(y)

    def project(out_ref, seg):
        for c0 in range(seg[0], seg[1], DOT_COLS):
            c1 = min(c0 + DOT_COLS, seg[1])
            out_ref[:, c0 - seg[0]:c1 - seg[0]] = _mm(y_hi, w_ref[:, c0:c1]).astype(out_ref.dtype)

    project(na_ref, SEG_NA)
    project(z_ref, SEG_Z)
    project(xbc_ref, SEG_XBC)
    project(gate_ref, SEG_GATE)
    dt_ref[...] = _mm(y_hi, wdh_ref[...]) + _mm(y_lo, wdh_ref[...]) + _mm(y_hi, wdl_ref[...])

    cos = cos_ref[...]
    sin = sin_ref[...]
    seg_ones = seg_ref[...]
    for pair in range(GQA_Q // LANES):
        c0 = SEG_GQ[0] + pair * LANES
        xr = _qk_norm_rope(_mm(y_hi, w_ref[:, c0:c0 + LANES]), qw_ref[...], cos, sin, seg_ones)
        xr = (xr * HEAD_DIM ** -0.5).astype(BF16)
        q_ref[2 * pair] = xr[:, :HEAD_DIM]
        q_ref[2 * pair + 1] = xr[:, HEAD_DIM:]
    kr = _qk_norm_rope(_mm(y_hi, w_ref[:, SEG_GK[0]:SEG_GK[1]]), kw_ref[...], cos, sin, seg_ones).astype(BF16)
    k_ref[0] = kr[:, :HEAD_DIM]
    k_ref[1] = kr[:, HEAD_DIM:]
    vv = _mm(y_hi, w_ref[:, SEG_GV[0]:SEG_GV[1]]).astype(BF16)
    v_ref[0] = vv[:, :HEAD_DIM]
    v_ref[1] = vv[:, HEAD_DIM:]


def _in_proj(h, mod4, row_fn, norm_w, cos, sin, qw, kw, seg_ones, w_main, w_dt_hi, w_dt_lo, tm, seq_len):
    n_tok = h.shape[0]
    pos_tiles = seq_len // tm
    tok = lambda width: pl.BlockSpec((tm, width), lambda i: (i, 0))
    hm = lambda nh: pl.BlockSpec((nh, tm, HEAD_DIM), lambda i: (0, i, 0))
    outs = pl.pallas_call(
        _inproj_kernel,
        out_shape=(
            jax.ShapeDtypeStruct((n_tok, 3 * NA_W), BF16),
            jax.ShapeDtypeStruct((n_tok, SSM_D_INNER), BF16),
            jax.ShapeDtypeStruct((n_tok, SSM_XBC), BF16),
            jax.ShapeDtypeStruct((n_tok, DT_COLS), F32),
            jax.ShapeDtypeStruct((GQA_HEADS, n_tok, HEAD_DIM), BF16),
            jax.ShapeDtypeStruct((GQA_KV_HEADS, n_tok, HEAD_DIM), BF16),
            jax.ShapeDtypeStruct((GQA_KV_HEADS, n_tok, HEAD_DIM), BF16),
            jax.ShapeDtypeStruct((n_tok, 3 * D_MODEL), BF16),
        ),
        grid=(n_tok // tm,),
        in_specs=[
            tok(D_MODEL),
            _mod_spec(0, row_fn),
            _mod_spec(1, row_fn),
            _resident((1, D_MODEL)),
            pl.BlockSpec((tm, LANES), lambda i: (i % pos_tiles, 0)),
            pl.BlockSpec((tm, LANES), lambda i: (i % pos_tiles, 0)),
            _resident((1, LANES)),
            _resident((1, LANES)),
            _resident((LANES, LANES)),
            _resident((D_MODEL, W_MAIN_COLS)),
            _resident((D_MODEL, DT_COLS)),
            _resident((D_MODEL, DT_COLS)),
        ],
        out_specs=(tok(3 * NA_W), tok(SSM_D_INNER), tok(SSM_XBC), tok(DT_COLS),
                   hm(GQA_HEADS), hm(GQA_KV_HEADS), hm(GQA_KV_HEADS), tok(3 * D_MODEL)),
        compiler_params=_params(("arbitrary",)),
        name="in_proj",
    )(h, mod4, mod4, norm_w, cos, sin, qw, kw, seg_ones, w_main, w_dt_hi, w_dt_lo)
    return outs


def _softmax_pv(s_list, v_list):
    m = s_list[0].max(axis=-1, keepdims=True)
    for s in s_list[1:]:
        m = jnp.maximum(m, s.max(axis=-1, keepdims=True))
    den = 0.0
    out = 0.0
    for s, v in zip(s_list, v_list):
        p = jnp.exp(s - m)
        den = den + p.sum(axis=-1, keepdims=True)
        out = out + _mm(p.astype(BF16), v)
    return out / den


def _na_kernel(q_ref, k_ref, v_ref, kc_ref, vc_ref, bm_ref, o_ref, *, n_rows):
    win = NA_WIN_ROWS * GRID_W
    kc = kc_ref[...]
    vc = vc_ref[...]
    head0 = lax.broadcasted_iota(jnp.int32, (1, LANES), 1) < HEAD_DIM

    def row(r, carry):
        rs = jnp.clip(r - NA_WIN_ROWS // 2, 0, n_rows - NA_WIN_ROWS)
        off = rs - r + NA_WIN_ROWS - 1
        q2 = q_ref[pl.ds(pl.multiple_of(r * GRID_W, GRID_W), GRID_W), :]
        ks = k_ref[pl.ds(pl.multiple_of(rs * GRID_W, GRID_W), win), :]
        vs = v_ref[pl.ds(pl.multiple_of(rs * GRID_W, GRID_W), win), :]
        outs = []
        for hh in range(2):
            keep = head0 if hh == 0 else jnp.logical_not(head0)
            qm = jnp.where(keep, q2, jnp.zeros_like(q2))
            s_win = _mm_nt(qm, ks) + bm_ref[hh, off]
            s_ctx = _mm_nt(qm, kc)
            outs.append(_softmax_pv([s_win, s_ctx], [vs, vc]))
        o_ref[pl.ds(pl.multiple_of(r * GRID_W, GRID_W), GRID_W), :] = jnp.where(head0, outs[0], outs[1]).astype(BF16)
        return carry

    lax.fori_loop(0, n_rows, row, 0)


def _neighbourhood_attention(na_l, na_c, bias_mask, n_batch, seq_len, ctx_len):
    n_rows = seq_len // GRID_W
    n_pairs = NA_W // LANES
    lat = lambda part: pl.BlockSpec((seq_len, LANES), lambda b, p: (b, part * n_pairs + p))
    ctx = lambda part: pl.BlockSpec((ctx_len, LANES), lambda b, p: (b, part * n_pairs + p))
    return pl.pallas_call(
        functools.partial(_na_kernel, n_rows=n_rows),
        out_shape=jax.ShapeDtypeStruct((n_batch * seq_len, NA_W), BF16),
        grid=(n_batch, n_pairs),
        in_specs=[lat(0), lat(1), lat(2), ctx(1), ctx(2),
                  pl.BlockSpec((2, NA_WIN_ROWS, GRID_W, NA_WIN_ROWS * GRID_W), lambda b, p: (p, 0, 0, 0))],
        out_specs=pl.BlockSpec((seq_len, LANES), lambda b, p: (b, p)),
        compiler_params=_params(("arbitrary", "arbitrary")),
        name="neighbourhood_attention",
    )(na_l, na_l, na_l, na_c, na_c, bias_mask)


def _ctx_mha_kernel(q_ref, k_ref, v_ref, o_ref):
    q2 = q_ref[...]
    k2 = k_ref[...]
    v2 = v_ref[...]
    head0 = lax.broadcasted_iota(jnp.int32, (1, LANES), 1) < HEAD_DIM
    outs = []
    for hh in range(2):
        keep = head0 if hh == 0 else jnp.logical_not(head0)
        qm = jnp.where(keep, q2, jnp.zeros_like(q2))
        outs.append(_softmax_pv([_mm_nt(qm, k2)], [v2]))
    o_ref[...] = jnp.where(head0, outs[0], outs[1]).astype(BF16)


def _context_mha(na_c, n_batch, ctx_len):
    n_pairs = NA_W // LANES
    part = lambda which: pl.BlockSpec((ctx_len, LANES), lambda b, p: (b, which * n_pairs + p))
    return pl.pallas_call(
        _ctx_mha_kernel,
        out_shape=jax.ShapeDtypeStruct((n_batch * ctx_len, NA_W), BF16),
        grid=(n_batch, n_pairs),
        in_specs=[part(0), part(1), part(2)],
        out_specs=pl.BlockSpec((ctx_len, LANES), lambda b, p: (b, p)),
        compiler_params=_params(("arbitrary", "arbitrary")),
        name="context_mha",
    )(na_c, na_c, na_c)


def _gqa_kernel(*refs, tq, bk, n_lat_blocks):
    if n_lat_blocks:
        q_ref, kc_ref, vc_ref, kl_ref, vl_ref, o_ref = refs
    else:
        q_ref, kc_ref, vc_ref, o_ref = refs
    q = q_ref[...].reshape(GQA_GROUP * tq, HEAD_DIM)
    s = _mm_nt(q, kc_ref[...])
    m = s.max(axis=-1, keepdims=True)
    p = jnp.exp(s - m)
    den = p.sum(axis=-1, keepdims=True)
    acc = _mm(p.astype(BF16), vc_ref[...])

    def block(j, carry):
        m_old, den_old, acc_old = carry
        start = pl.multiple_of(j * bk, bk)
        sj = _mm_nt(q, kl_ref[pl.ds(start, bk), :])
        m_new = jnp.maximum(m_old, sj.max(axis=-1, keepdims=True))
        alpha = jnp.exp(m_old - m_new)
        pj = jnp.exp(sj - m_new)
        den_new = alpha * den_old + pj.sum(axis=-1, keepdims=True)
        acc_new = alpha * acc_old + _mm(pj.astype(BF16), vl_ref[pl.ds(start, bk), :])
        return m_new, den_new, acc_new

    if n_lat_blocks:
        m, den, acc = lax.fori_loop(0, n_lat_blocks, block, (m, den, acc))
    o = acc / den
    o_ref[...] = jnp.concatenate([o[g * tq:(g + 1) * tq] for g in range(GQA_GROUP)], axis=-1).astype(BF16)


def _gqa_attention(q_hm, k_c, v_c, k_l, v_l, n_batch, q_len, ctx_len, tq, bk):
    q_tiles = q_len // tq
    in_specs = [
        pl.BlockSpec((GQA_GROUP, tq, HEAD_DIM), lambda b, g, i: (g, b * q_tiles + i, 0)),
        pl.BlockSpec((None, ctx_len, HEAD_DIM), lambda b, g, i: (g, b, 0)),
        pl.BlockSpec((None, ctx_len, HEAD_DIM), lambda b, g, i: (g, b, 0)),
    ]
    args = [q_hm, k_c, v_c]
    n_lat_blocks = 0
    if k_l is not None:
        lat_len = k_l.shape[1] // n_batch
        n_lat_blocks = lat_len // bk
        in_specs += [pl.BlockSpec((None, lat_len, HEAD_DIM), lambda b, g, i: (g, b, 0))] * 2
        args += [k_l, v_l]
    return pl.pallas_call(
        functools.partial(_gqa_kernel, tq=tq, bk=bk, n_lat_blocks=n_lat_blocks),
        out_shape=jax.ShapeDtypeStruct((n_batch * q_len, GQA_Q), BF16),
        grid=(n_batch, GQA_KV_HEADS, q_tiles),
        in_specs=in_specs,
        out_specs=pl.BlockSpec((tq, GQA_GROUP * HEAD_DIM), lambda b, g, i: (b * q_tiles + i, g)),
        compiler_params=_params(("arbitrary", "arbitrary", "arbitrary")),
        name="gqa_attention",
    )(*args)


def _ssd_kernel(xl_ref, xlp_ref, xln_ref, xc_ref, xcp_ref, xcn_ref, dtl_ref, dtc_ref, zl_ref, zc_ref,
                cw_ref, cb_ref, alog_ref, dtb_ref, dsk_ref, nw_ref, tri_ref, shift_ref,
                outl_ref, outc_ref, x_sc, dt_sc, z_sc, y_sc, st_sc, *, n_chunks, n_ctx_chunks):
    T = SSM_CHUNK
    s = pl.program_id(1)
    direction = s // n_chunks
    sp = s % n_chunks
    gid = jnp.where(direction == 0, sp,
                    jnp.where(sp < n_ctx_chunks, n_ctx_chunks - 1 - sp, n_chunks + n_ctx_chunks - 1 - sp))
    is_ctx = gid < n_ctx_chunks
    seg_first = jnp.logical_or(gid == 0, gid == n_ctx_chunks)
    seg_last = jnp.logical_or(gid == n_ctx_chunks - 1, gid == n_chunks - 1)

    def stage(x_ref, xp_ref, xn_ref, dt_ref, z_ref):
        x_sc[0:BF16_ROWS] = jnp.where(seg_first, jnp.zeros_like(xp_ref[...]), xp_ref[...])
        x_sc[BF16_ROWS:BF16_ROWS + T] = x_ref[...]
        x_sc[BF16_ROWS + T:] = jnp.where(seg_last, jnp.zeros_like(xn_ref[...]), xn_ref[...])
        dt_sc[...] = dt_ref[...]
        z_sc[...] = z_ref[...]

    @pl.when(is_ctx)
    def _():
        stage(xc_ref, xcp_ref, xcn_ref, dtc_ref, zc_ref)

    @pl.when(jnp.logical_not(is_ctx))
    def _():
        stage(xl_ref, xlp_ref, xln_ref, dtl_ref, zl_ref)

    @pl.when(sp == 0)
    def _():
        st_sc[...] = jnp.zeros_like(st_sc)

    ext = x_sc[...]
    x_mid = ext[BF16_ROWS:BF16_ROWS + T].astype(F32)
    x_dn = _mm(shift_ref[0], ext)
    x_up = _mm(shift_ref[1], ext)
    xbc = _silu(cw_ref[0:1] * x_dn + cw_ref[1:2] * x_mid + cw_ref[2:3] * x_up + cb_ref[...])
    xs = xbc[:, :SSM_D_INNER]

    dt = _softplus(dt_sc[...] + dtb_ref[...])
    da = dt * (-jnp.exp(alog_ref[...]))
    tri = tri_ref[direction]
    tri_t = tri_ref[1 - direction]
    da_hi, da_lo = _split_bf16(da)
    cs = _mm(tri, da_hi) + _mm(tri, da_lo)
    da_t = da.T
    dat_hi, dat_lo = _split_bf16(da_t)
    cs_t = _mm(dat_hi, tri_t) + _mm(dat_lo, tri_t)
    tot_t = da_t.sum(axis=-1, keepdims=True)
    dt_t = dt.T
    w_t = dt_t * jnp.exp(tot_t - cs_t)
    causal = tri > 0

    ys = []
    for g in range(SSM_GROUPS):
        b_g = xbc[:, SSM_D_INNER + g * SSM_D_STATE:SSM_D_INNER + (g + 1) * SSM_D_STATE]
        c_g = xbc[:, SSM_D_INNER + SSM_BC + g * SSM_D_STATE:SSM_D_INNER + SSM_BC + (g + 1) * SSM_D_STATE]
        cb = _mm_nt(c_g.astype(BF16), b_g.astype(BF16))
        b_gt = b_g.T
        for r in range(SSM_HEADS // SSM_GROUPS):
            h = g * (SSM_HEADS // SSM_GROUPS) + r
            x_h = xs[:, h * HEAD_DIM:(h + 1) * HEAD_DIM].astype(BF16)
            col = cs[:, h:h + 1]
            decay = jnp.exp(jnp.where(causal, col - cs_t[h:h + 1, :], NEG_BIG))
            m_h = (cb * decay * dt_t[h:h + 1, :]).astype(BF16)
            st = st_sc[h]
            y_h = _mm(m_h, x_h) + _mm((c_g * jnp.exp(col)).astype(BF16), st.astype(BF16))
            ys.append(y_h)
            st_sc[h] = jnp.exp(tot_t[h:h + 1, :]) * st + _mm((b_gt * w_t[h:h + 1, :]).astype(BF16), x_h)
    y = jnp.concatenate(ys, axis=-1)

    @pl.when(direction == 0)
    def _():
        y_sc[gid] = y + dsk_ref[...] * xs

    @pl.when(direction == 1)
    def _():
        gated = (y_sc[gid] + y) * _silu(z_sc[...].astype(F32))
        res = (_rms(gated) * nw_ref[...]).astype(BF16)

        @pl.when(is_ctx)
        def _():
            outc_ref[...] = res

        @pl.when(jnp.logical_not(is_ctx))
        def _():
            outl_ref[...] = res


def _ssd_mixer(xbc_l, dt_l, z_l, xbc_c, dt_c, z_c, conv_w8, conv_b, a_log_p, dt_bias_p, d_skip_x, norm_w,
               tri, shift, n_batch, seq_len, ctx_len):
    T = SSM_CHUNK
    nlc = seq_len // T
    ncc = ctx_len // T
    n_chunks = nlc + ncc
    halo = T // BF16_ROWS

    def ids(s):
        direction = s // n_chunks
        sp = s % n_chunks
        gid = jnp.where(direction == 0, sp, jnp.where(sp < ncc, ncc - 1 - sp, n_chunks + ncc - 1 - sp))
        return direction, gid, jnp.clip(gid, 0, ncc - 1), jnp.clip(gid - ncc, 0, nlc - 1)

    def lat_main(b, s):
        return (b * nlc + ids(s)[3], 0)

    def lat_prev(b, s):
        return (jnp.maximum((b * nlc + ids(s)[3]) * halo - 1, 0), 0)

    def lat_next(b, s):
        return (jnp.minimum((b * nlc + ids(s)[3] + 1) * halo, n_batch * nlc * halo - 1), 0)

    def ctx_main(b, s):
        return (b * ncc + ids(s)[2], 0)

    def ctx_prev(b, s):
        return (jnp.maximum((b * ncc + ids(s)[2]) * halo - 1, 0), 0)

    def ctx_next(b, s):
        return (jnp.minimum((b * ncc + ids(s)[2] + 1) * halo, n_batch * ncc * halo - 1), 0)

    def lat_dt(b, s):
        return (b * nlc + ids(s)[3], ids(s)[0])

    def ctx_dt(b, s):
        return (b * ncc + ids(s)[2], ids(s)[0])

    def lat_out(b, s):
        direction, gid, _, lc = ids(s)
        return (b * nlc + jnp.where(jnp.logical_and(direction == 1, gid >= ncc), lc, nlc - 1), 0)

    def ctx_out(b, s):
        direction, gid, cc, _ = ids(s)
        return (b * ncc + jnp.where(direction == 1, jnp.where(gid < ncc, cc, 0), ncc - 1), 0)

    def by_dir(b, s):
        return (ids(s)[0], 0, 0)

    const2 = lambda b, s: (0, 0)
    const3 = lambda b, s: (0, 0, 0)
    return pl.pallas_call(
        functools.partial(_ssd_kernel, n_chunks=n_chunks, n_ctx_chunks=ncc),
        out_shape=(jax.ShapeDtypeStruct((n_batch * seq_len, SSM_D_INNER), BF16),
                   jax.ShapeDtypeStruct((n_batch * ctx_len, SSM_D_INNER), BF16)),
        grid=(n_batch, 2 * n_chunks),
        in_specs=[
            pl.BlockSpec((T, SSM_XBC), lat_main),
            pl.BlockSpec((BF16_ROWS, SSM_XBC), lat_prev),
            pl.BlockSpec((BF16_ROWS, SSM_XBC), lat_next),
            pl.BlockSpec((T, SSM_XBC), ctx_main),
            pl.BlockSpec((BF16_ROWS, SSM_XBC), ctx_prev),
            pl.BlockSpec((BF16_ROWS, SSM_XBC), ctx_next),
            pl.BlockSpec((T, LANES), lat_dt),
            pl.BlockSpec((T, LANES), ctx_dt),
            pl.BlockSpec((T, SSM_D_INNER), lat_out),
            pl.BlockSpec((T, SSM_D_INNER), ctx_out),
            pl.BlockSpec((F32_ROWS, SSM_XBC), const2),
            pl.BlockSpec((1, SSM_XBC), const2),
            pl.BlockSpec((None, 1, LANES), by_dir),
            pl.BlockSpec((None, 1, LANES), by_dir),
            pl.BlockSpec((1, SSM_D_INNER), const2),
            pl.BlockSpec((1, SSM_D_INNER), const2),
            pl.BlockSpec((2, T, T), const3),
            pl.BlockSpec((2, T, T + 2 * BF16_ROWS), const3),
        ],
        out_specs=(pl.BlockSpec((T, SSM_D_INNER), lat_out), pl.BlockSpec((T, SSM_D_INNER), ctx_out)),
        scratch_shapes=[
            pltpu.VMEM((T + 2 * BF16_ROWS, SSM_XBC), BF16),
            pltpu.VMEM((T, LANES), F32),
            pltpu.VMEM((T, SSM_D_INNER), BF16),
            pltpu.VMEM((n_chunks, T, SSM_D_INNER), F32),
            pltpu.VMEM((SSM_HEADS, SSM_D_STATE, HEAD_DIM), F32),
        ],
        compiler_params=_params(("arbitrary", "arbitrary")),
        name="ssd_mixer",
    )(xbc_l, xbc_l, xbc_l, xbc_c, xbc_c, xbc_c, dt_l, dt_c, z_l, z_c,
      conv_w8, conv_b, a_log_p, dt_bias_p, d_skip_x, norm_w, tri, shift)


def _merge_kernel(h_ref, na_ref, ssm_ref, gqa_ref, gate_ref, g_ref, w1_ref, w2_ref, w3_ref, wo_ref, o_ref):
    gates = gate_ref[...]
    t = _sigmoid(gates[:, :D_MODEL].astype(F32)) * _mm(na_ref[...], w1_ref[...])
    t = t + _sigmoid(gates[:, D_MODEL:2 * D_MODEL].astype(F32)) * _mm(ssm_ref[...], w2_ref[...])
    t = t + _sigmoid(gates[:, 2 * D_MODEL:].astype(F32)) * _mm(gqa_ref[...], w3_ref[...])
    o_ref[...] = h_ref[...] + g_ref[...] * _mm(t.astype(BF16), wo_ref[...])


def _merge(h, o_na, o_ssm, o_gqa, gates, mod4, row_fn, w1, w2, w3, wo, tm):
    n_tok = h.shape[0]
    tok = lambda width: pl.BlockSpec((tm, width), lambda i: (i, 0))
    return pl.pallas_call(
        _merge_kernel,
        out_shape=jax.ShapeDtypeStruct((n_tok, D_MODEL), F32),
        grid=(n_tok // tm,),
        in_specs=[tok(D_MODEL), tok(NA_W), tok(SSM_D_INNER), tok(GQA_Q), tok(3 * D_MODEL),
                  _mod_spec(2, row_fn),
                  _resident((NA_W, D_MODEL)), _resident((SSM_D_INNER, D_MODEL)),
                  _resident((GQA_Q, D_MODEL)), _resident((D_MODEL, D_MODEL))],
        out_specs=tok(D_MODEL),
        compiler_params=_params(("arbitrary",)),
        name="branch_merge",
    )(h, o_na, o_ssm, o_gqa, gates, mod4, w1, w2, w3, wo)


def _ffn_kernel(h_ref, hp_ref, hn_ref, sh_ref, sc_ref, g_ref, nw_ref, wup_ref, cw_ref, wdn_ref, fw_ref,
                o_ref, hn_sc, acc_sc, *, tm, tiles_per_seq, final_norm):
    j = pl.program_id(0) % tiles_per_seq
    seq_first = j == 0
    seq_last = j == tiles_per_seq - 1

    def norm_mod(x):
        return (_rms(x) * nw_ref[...]) * (1.0 + sc_ref[...]) + sh_ref[...]

    x = h_ref[...]
    hn_sc[0:tm] = norm_mod(x).astype(BF16)
    before = jnp.where(seq_first, 0.0, norm_mod(hp_ref[...]))
    after = jnp.where(seq_last, 0.0, norm_mod(hn_ref[...]))
    hn_sc[tm:] = jnp.concatenate([before, after], axis=0).astype(BF16)
    acc_sc[...] = jnp.zeros_like(acc_sc)
    row = lax.broadcasted_iota(jnp.int32, (tm, 1), 0)

    def chunk(c, carry):
        u = _mm(hn_sc[...], wup_ref[c])
        u_mid = u[0:tm]
        u_before = u[tm + F32_ROWS - 1:tm + F32_ROWS]
        u_after = u[tm + F32_ROWS:tm + F32_ROWS + 1]
        u_dn = jnp.where(row == 0, u_before, pltpu.roll(u_mid, 1, 0))
        u_up = jnp.where(row == tm - 1, u_after, pltpu.roll(u_mid, tm - 1, 0))
        cw = cw_ref[c]
        y = cw[0:1] * u_dn + cw[1:2] * u_mid + cw[2:3] * u_up + cw[3:4]
        act = (_silu(y[:, :FFN_TN]) * y[:, FFN_TN:]).astype(BF16)
        acc_sc[...] += _mm(act, wdn_ref[c])
        return carry

    lax.fori_loop(0, FFN_CHUNKS, chunk, 0)
    out = x + g_ref[...] * acc_sc[...]
    if final_norm:
        out = _rms(out) * fw_ref[...]
    o_ref[...] = out


def _conv_ffn(h, mod4, row_fn, norm_w, w_up_c, conv_c, w_down_c, final_w, tm, seq_len, final_norm):
    n_tok = h.shape[0]
    halo = tm // F32_ROWS
    n_halo = n_tok // F32_ROWS
    return pl.pallas_call(
        functools.partial(_ffn_kernel, tm=tm, tiles_per_seq=seq_len // tm, final_norm=final_norm),
        out_shape=jax.ShapeDtypeStruct((n_tok, D_MODEL), F32),
        grid=(n_tok // tm,),
        in_specs=[
            pl.BlockSpec((tm, D_MODEL), lambda i: (i, 0)),
            pl.BlockSpec((F32_ROWS, D_MODEL), lambda i: (jnp.maximum(i * halo - 1, 0), 0)),
            pl.BlockSpec((F32_ROWS, D_MODEL), lambda i: (jnp.minimum((i + 1) * halo, n_halo - 1), 0)),
            _mod_spec(3, row_fn),
            _mod_spec(4, row_fn),
            _mod_spec(5, row_fn),
            _resident((1, D_MODEL)),
            _resident((FFN_CHUNKS, D_MODEL, 2 * FFN_TN)),
            _resident((FFN_CHUNKS, F32_ROWS, 2 * FFN_TN)),
            _resident((FFN_CHUNKS, FFN_TN, D_MODEL)),
            _resident((1, D_MODEL)),
        ],
        out_specs=pl.BlockSpec((tm, D_MODEL), lambda i: (i, 0)),
        scratch_shapes=[pltpu.VMEM((tm + 2 * F32_ROWS, D_MODEL), BF16), pltpu.VMEM((tm, D_MODEL), F32)],
        compiler_params=_params(("arbitrary",)),
        name="conv_ffn",
    )(h, h, h, mod4, mod4, mod4, norm_w, w_up_c, conv_c, w_down_c, final_w)


def _rope_tables(seq_len):
    n_freq = HEAD_DIM // 4
    inv_freq = ROPE_THETA ** (-jnp.arange(n_freq, dtype=F32) / n_freq)
    t = jnp.arange(seq_len, dtype=jnp.int32)
    row = (t // GRID_W).astype(F32)
    col = (t % GRID_W).astype(F32)
    ang = jnp.concatenate([row[:, None] * inv_freq, col[:, None] * inv_freq], axis=-1)
    cos = jnp.repeat(jnp.cos(ang), 2, axis=-1)
    sin = jnp.repeat(jnp.sin(ang), 2, axis=-1) * jnp.tile(jnp.asarray([-1.0, 1.0], F32), HEAD_DIM // 2)
    return jnp.tile(cos, (1, LANES // HEAD_DIM)), jnp.tile(sin, (1, LANES // HEAD_DIM))


def _na_bias_mask(rel_bias):
    qc = np.arange(GRID_W)[:, None]
    kc = np.arange(GRID_W)[None, :]
    win0 = np.clip(qc - NA_WIN_COLS // 2, 0, GRID_W - NA_WIN_COLS)
    ok = (kc >= win0) & (kc < win0 + NA_WIN_COLS)
    dcol = np.clip(kc - qc + NA_WIN_COLS - 1, 0, 2 * NA_WIN_COLS - 2)
    per_drow = jnp.where(jnp.asarray(ok)[None, None], rel_bias.astype(F32)[:, :, dcol], NEG_BIG)
    classes = [jnp.concatenate([per_drow[:, off + i] for i in range(NA_WIN_ROWS)], axis=-1)
               for off in range(NA_WIN_ROWS)]
    return jnp.stack(classes, axis=1)


def _pack_w_in(w_in):
    sizes = [3 * NA_W, SSM_D_INNER, SSM_XBC, 2 * SSM_HEADS, GQA_Q, GQA_KV, GQA_KV]
    na, z, xbc, dt, gq, gk, gv, gate = jnp.split(w_in, [int(i) for i in np.cumsum(sizes)], axis=-1)
    na = jnp.concatenate([na[:, :NA_W] * HEAD_DIM ** -0.5, na[:, NA_W:]], axis=-1)
    w_main = jnp.concatenate([na, z, xbc, gq, gk, gv, gate], axis=-1).astype(BF16)
    dt_pad = jnp.zeros((D_MODEL, DT_COLS), F32)
    dt_pad = dt_pad.at[:, :SSM_HEADS].set(dt[:, :SSM_HEADS]).at[:, LANES:LANES + SSM_HEADS].set(dt[:, SSM_HEADS:])
    dt_hi, dt_lo = _split_bf16(dt_pad)
    return w_main, dt_hi, dt_lo


def _pad_heads(v):
    return jnp.zeros((2, 1, LANES), F32).at[:, 0, :SSM_HEADS].set(v.astype(F32))


def _pack_ffn(w_up, conv_w, conv_b, w_down):
    def chunked(t):
        a, b = t[..., :FFN_HIDDEN], t[..., FFN_HIDDEN:]
        lead = t.shape[:-1]
        return jnp.concatenate([a.reshape(lead + (FFN_CHUNKS, FFN_TN)), b.reshape(lead + (FFN_CHUNKS, FFN_TN))], axis=-1)

    w_up_c = jnp.moveaxis(chunked(w_up), 1, 0).astype(BF16)
    taps = jnp.concatenate([conv_w, conv_b[None], jnp.zeros((F32_ROWS - 4, 2 * FFN_HIDDEN), F32)], axis=0)
    conv_c = jnp.moveaxis(chunked(taps), 1, 0)
    w_down_c = w_down.reshape(FFN_CHUNKS, FFN_TN, D_MODEL).astype(BF16)
    return w_up_c, conv_c, w_down_c


def _scan_constants():
    T = SSM_CHUNK
    t = np.arange(T)
    tril = (t[:, None] >= t[None, :]).astype(np.float32)
    tri = jnp.asarray(np.stack([tril, tril.T]), BF16)
    e = np.arange(T + 2 * BF16_ROWS)
    down = (e[None, :] == t[:, None] + BF16_ROWS - 1).astype(np.float32)
    up = (e[None, :] == t[:, None] + BF16_ROWS + 1).astype(np.float32)
    return tri, jnp.asarray(np.stack([down, up]), BF16)


def kernel(x, c, ctx, c_ctx, w_mod, b_mod, norm1_w, norm2_w, w_in, na_rel_bias, ssm_conv_w, ssm_conv_b, ssm_a_log, ssm_dt_bias, ssm_d, ssm_norm_w, q_norm_w, k_norm_w, w_out_na, w_out_ssm, w_out_gqa, w_o, ffn_w_up, ffn_conv_w, ffn_conv_b, ffn_w_down, final_norm_w):
    n_batch, seq_len, _ = x.shape
    ctx_len = ctx.shape[1]
    depth = w_mod.shape[0]
    assert seq_len % GRID_W == 0 and seq_len // GRID_W >= NA_WIN_ROWS
    assert n_batch <= CTX_MOD_ROW and seq_len % 512 == 0 and ctx_len % 256 == 0
    tm_lat, tm_ctx = 512, 256

    cc = jnp.zeros((MOD_ROWS, D_MODEL), F32).at[:n_batch].set(c).at[CTX_MOD_ROW].set(c_ctx)
    mods = _mod_vectors(cc, w_mod, b_mod)
    lat_row = lambda i: i // (seq_len // tm_lat)
    ctx_row = lambda i: CTX_MOD_ROW

    cos_l, sin_l = _rope_tables(seq_len)
    cos_c, sin_c = jnp.ones((ctx_len, LANES), F32), jnp.zeros((ctx_len, LANES), F32)
    head_of_lane = np.arange(LANES) // HEAD_DIM
    seg_ones = jnp.asarray(head_of_lane[:, None] == head_of_lane[None, :], BF16)
    tri, shift = _scan_constants()

    h_lat = x.reshape(n_batch * seq_len, D_MODEL)
    h_ctx = ctx.reshape(n_batch * ctx_len, D_MODEL)
    row = lambda v: v.reshape(1, -1).astype(F32)
    for layer in range(depth):
        need_ctx = layer < depth - 1
        last = layer == depth - 1
        mod4 = mods[layer].reshape(MOD_ROWS, N_MOD, 1, D_MODEL)
        w_main, w_dt_hi, w_dt_lo = _pack_w_in(w_in[layer])
        qw = row(jnp.tile(q_norm_w[layer], LANES // HEAD_DIM))
        kw = row(jnp.tile(k_norm_w[layer], LANES // HEAD_DIM))
        proj = functools.partial(_in_proj, mod4=mod4, norm_w=row(norm1_w[layer]), qw=qw, kw=kw, seg_ones=seg_ones,
                                 w_main=w_main, w_dt_hi=w_dt_hi, w_dt_lo=w_dt_lo)
        na_l, z_l, xbc_l, dt_l, q_l, k_l, v_l, gate_l = proj(
            h_lat, row_fn=lat_row, cos=cos_l, sin=sin_l, tm=tm_lat, seq_len=seq_len)
        na_c, z_c, xbc_c, dt_c, q_c, k_c, v_c, gate_c = proj(
            h_ctx, row_fn=ctx_row, cos=cos_c, sin=sin_c, tm=tm_ctx, seq_len=ctx_len)

        o_na_l = _neighbourhood_attention(na_l, na_c, _na_bias_mask(na_rel_bias[layer]), n_batch, seq_len, ctx_len)
        conv_w8 = jnp.concatenate([ssm_conv_w[layer], jnp.zeros((F32_ROWS - 3, SSM_XBC), F32)], axis=0)
        o_ssm_l, o_ssm_c = _ssd_mixer(
            xbc_l, dt_l, z_l, xbc_c, dt_c, z_c, conv_w8, row(ssm_conv_b[layer]), _pad_heads(ssm_a_log[layer]),
            _pad_heads(ssm_dt_bias[layer]), row(jnp.repeat(ssm_d[layer], HEAD_DIM)), row(ssm_norm_w[layer]),
            tri, shift, n_batch, seq_len, ctx_len)
        o_gqa_l = _gqa_attention(q_l, k_c, v_c, k_l, v_l, n_batch, seq_len, ctx_len, tq=256, bk=512)

        w1, w2, w3, wo = (w.astype(BF16) for w in (w_out_na[layer], w_out_ssm[layer], w_out_gqa[layer], w_o[layer]))
        ffn_w = _pack_ffn(ffn_w_up[layer], ffn_conv_w[layer], ffn_conv_b[layer], ffn_w_down[layer])
        h_lat = _merge(h_lat, o_na_l, o_ssm_l, o_gqa_l, gate_l, mod4, lat_row, w1, w2, w3, wo, tm_lat)
        h_lat = _conv_ffn(h_lat, mod4, lat_row, row(norm2_w[layer]), *ffn_w, row(final_norm_w),
                          tm=tm_lat, seq_len=seq_len, final_norm=last)
        if need_ctx:
            o_na_c = _context_mha(na_c, n_batch, ctx_len)
            o_gqa_c = _gqa_attention(q_c, k_c, v_c, None, None, n_batch, ctx_len, ctx_len, tq=ctx_len, bk=512)
            h_ctx = _merge(h_ctx, o_na_c, o_ssm_c, o_gqa_c, gate_c, mod4, ctx_row, w1, w2, w3, wo, tm_ctx)
            h_ctx = _conv_ffn(h_ctx, mod4, ctx_row, row(norm2_w[layer]), *ffn_w, row(final_norm_w),
                              tm=tm_ctx, seq_len=ctx_len, final_norm=False)
    return h_lat.reshape(n_batch, seq_len, D_MODEL)
```

```python
import functools

import numpy as np
import jax
import jax.numpy as jnp
from jax import lax
from jax.experimental import pallas as pl
from jax.experimental.pallas import tpu as pltpu

F32 = jnp.float32
BF16 = jnp.bfloat16

D_MODEL = 1024
GRID_W = 64
EPS = 1e-6
HEAD_DIM = 64
NA_HEADS = 8
NA_W = NA_HEADS * HEAD_DIM
NA_WIN_ROWS = 8
NA_WIN_COLS = 16
NA_ROW_GROUP = 4
NA_SLAB_ROWS = 12
SSM_HEADS = 16
SSM_D_INNER = SSM_HEADS * HEAD_DIM
SSM_GROUPS = 2
SSM_D_STATE = 64
SSM_BC = SSM_GROUPS * SSM_D_STATE
SSM_XBC = SSM_D_INNER + 2 * SSM_BC
SSM_CHUNK = 128
GQA_HEADS = 8
GQA_KV_HEADS = 2
GQA_GROUP = GQA_HEADS // GQA_KV_HEADS
GQA_Q = GQA_HEADS * HEAD_DIM
GQA_KV = GQA_KV_HEADS * HEAD_DIM
GQA_BK = 256
GQA_ROW_CHUNK = 128
ROPE_THETA = 10000.0
FFN_HIDDEN = 2816
N_MOD = 6
MOD_ROWS = 16
CTX_MOD_ROW = 8
NEG_BIG = -1e30

LANES = 128
BF16_ROWS = 16
F32_ROWS = 8
VMEM_LIMIT = 56 * 1024 * 1024

SEG_NA = (0, 3 * NA_W)
SEG_Z = (SEG_NA[1], SEG_NA[1] + SSM_D_INNER)
SEG_XBC = (SEG_Z[1], SEG_Z[1] + SSM_XBC)
SEG_GQ = (SEG_XBC[1], SEG_XBC[1] + GQA_Q)
SEG_GK = (SEG_GQ[1], SEG_GQ[1] + GQA_KV)
SEG_GV = (SEG_GK[1], SEG_GK[1] + GQA_KV)
SEG_GATE = (SEG_GV[1], SEG_GV[1] + 3 * D_MODEL)
W_MAIN_COLS = SEG_GATE[1]
DT_COLS = 2 * LANES
DOT_COLS = 512

FFN_TN = 256
FFN_CHUNKS = FFN_HIDDEN // FFN_TN


def _mm(a, b):
    return jnp.dot(a, b, preferred_element_type=F32)


def _mm_nt(a, b):
    return lax.dot_general(a, b, (((1,), (1,)), ((), ())), preferred_element_type=F32)


def _split_bf16(x):
    hi = x.astype(BF16)
    lo = (x - hi.astype(F32)).astype(BF16)
    return hi, lo


def _sigmoid(x):
    return 1.0 / (1.0 + jnp.exp(-x))


def _silu(x):
    return x * _sigmoid(x)


def _softplus(x):
    return jnp.maximum(x, 0.0) + jnp.log1p(jnp.exp(-jnp.abs(x)))


def _rms(x):
    return x * lax.rsqrt(jnp.mean(x * x, axis=-1, keepdims=True) + EPS)


def _resident(shape):
    nd = len(shape)
    return pl.BlockSpec(shape, lambda *_: (0,) * nd, pipeline_mode=pl.Buffered(1))


def _params(sem):
    return pltpu.CompilerParams(dimension_semantics=sem, vmem_limit_bytes=VMEM_LIMIT)


def _mod_kernel(c_ref, w_ref, b_ref, o_ref):
    x_hi, x_lo = _split_bf16(_silu(c_ref[...]))
    w_hi, w_lo = _split_bf16(w_ref[...])
    o_ref[...] = _mm(x_hi, w_hi) + _mm(x_lo, w_hi) + _mm(x_hi, w_lo) + b_ref[...]


def _mod_vectors(cc, w_mod, b_mod):
    n_layers = w_mod.shape[0]
    tn = D_MODEL
    return pl.pallas_call(
        _mod_kernel,
        out_shape=jax.ShapeDtypeStruct((n_layers, MOD_ROWS, N_MOD * D_MODEL), F32),
        grid=(n_layers, N_MOD),
        in_specs=[
            pl.BlockSpec((MOD_ROWS, D_MODEL), lambda l, j: (0, 0)),
            pl.BlockSpec((None, D_MODEL, tn), lambda l, j: (l, 0, j)),
            pl.BlockSpec((None, 1, tn), lambda l, j: (l, 0, j)),
        ],
        out_specs=pl.BlockSpec((None, MOD_ROWS, tn), lambda l, j: (l, 0, j)),
        compiler_params=_params(("arbitrary", "arbitrary")),
        name="mod_vectors",
    )(cc, w_mod, b_mod.reshape(n_layers, 1, N_MOD * D_MODEL))


def _mod_spec(which, row_fn):
    return pl.BlockSpec((None, None, 1, D_MODEL), lambda i: (row_fn(i), which, 0, 0))


def _qk_norm_rope(x, w, cos, sin, seg_ones):
    sq_hi, sq_lo = _split_bf16(x * x)
    ss = _mm(sq_hi, seg_ones) + _mm(sq_lo, seg_ones)
    xn = x * lax.rsqrt(ss * (1.0 / HEAD_DIM) + EPS) * w
    lane = lax.broadcasted_iota(jnp.int32, xn.shape, 1)
    partner = jnp.where((lane & 1) == 0, pltpu.roll(xn, LANES - 1, 1), pltpu.roll(xn, 1, 1))
    return xn * cos + partner * sin


def _inproj_kernel(h_ref, sh_ref, sc_ref, nw_ref, cos_ref, sin_ref, qw_ref, kw_ref, seg_ref,
                   w_ref, wdh_ref, wdl_ref,
                   na_ref, z_ref, xbc_ref, dt_ref, q_ref, k_ref, v_ref, gate_ref):
    y = _rms(h_ref[...]) * nw_ref[...]
    y = y * (1.0 + sc_ref[...]) + sh_ref[...]
    y_hi, y_lo = _split_bf16(y)

    def project(out_ref, seg):
        for c0 in range(seg[0], seg[1], DOT_COLS):
            c1 = min(c0 + DOT_COLS, seg[1])
            out_ref[:, c0 - seg[0]:c1 - seg[0]] = _mm(y_hi, w_ref[:, c0:c1]).astype(out_ref.dtype)

    project(na_ref, SEG_NA)
    project(z_ref, SEG_Z)
    project(xbc_ref, SEG_XBC)
    project(gate_ref, SEG_GATE)
    dt_ref[...] = _mm(y_hi, wdh_ref[...]) + _mm(y_lo, wdh_ref[...]) + _mm(y_hi, wdl_ref[...])

    cos = cos_ref[...]
    sin = sin_ref[...]
    seg_ones = seg_ref[...]
    for pair in range(GQA_Q // LANES):
        c0 = SEG_GQ[0] + pair * LANES
        xr = _qk_norm_rope(_mm(y_hi, w_ref[:, c0:c0 + LANES]), qw_ref[...], cos, sin, seg_ones)
        xr = (xr * HEAD_DIM ** -0.5).astype(BF16)
        q_ref[2 * pair] = xr[:, :HEAD_DIM]
        q_ref[2 * pair + 1] = xr[:, HEAD_DIM:]
    kr = _qk_norm_rope(_mm(y_hi, w_ref[:, SEG_GK[0]:SEG_GK[1]]), kw_ref[...], cos, sin, seg_ones).astype(BF16)
    k_ref[0] = kr[:, :HEAD_DIM]
    k_ref[1] = kr[:, HEAD_DIM:]
    vv = _mm(y_hi, w_ref[:, SEG_GV[0]:SEG_GV[1]]).astype(BF16)
    v_ref[0] = vv[:, :HEAD_DIM]
    v_ref[1] = vv[:, HEAD_DIM:]


def _in_proj(h, mod4, row_fn, norm_w, cos, sin, qw, kw, seg_ones, w_main, w_dt_hi, w_dt_lo, tm, seq_len):
    n_tok = h.shape[0]
    pos_tiles = seq_len // tm
    tok = lambda width: pl.BlockSpec((tm, width), lambda i: (i, 0))
    hm = lambda nh: pl.BlockSpec((nh, tm, HEAD_DIM), lambda i: (0, i, 0))
    outs = pl.pallas_call(
        _inproj_kernel,
        out_shape=(
            jax.ShapeDtypeStruct((n_tok, 3 * NA_W), BF16),
            jax.ShapeDtypeStruct((n_tok, SSM_D_INNER), BF16),
            jax.ShapeDtypeStruct((n_tok, SSM_XBC), BF16),
            jax.ShapeDtypeStruct((n_tok, DT_COLS), F32),
            jax.ShapeDtypeStruct((GQA_HEADS, n_tok, HEAD_DIM), BF16),
            jax.ShapeDtypeStruct((GQA_KV_HEADS, n_tok, HEAD_DIM), BF16),
            jax.ShapeDtypeStruct((GQA_KV_HEADS, n_tok, HEAD_DIM), BF16),
            jax.ShapeDtypeStruct((n_tok, 3 * D_MODEL), BF16),
        ),
        grid=(n_tok // tm,),
        in_specs=[
            tok(D_MODEL),
            _mod_spec(0, row_fn),
            _mod_spec(1, row_fn),
            _resident((1, D_MODEL)),
            pl.BlockSpec((tm, LANES), lambda i: (i % pos_tiles, 0)),
            pl.BlockSpec((tm, LANES), lambda i: (i % pos_tiles, 0)),
            _resident((1, LANES)),
            _resident((1, LANES)),
            _resident((LANES, LANES)),
            _resident((D_MODEL, W_MAIN_COLS)),
            _resident((D_MODEL, DT_COLS)),
            _resident((D_MODEL, DT_COLS)),
        ],
        out_specs=(tok(3 * NA_W), tok(SSM_D_INNER), tok(SSM_XBC), tok(DT_COLS),
                   hm(GQA_HEADS), hm(GQA_KV_HEADS), hm(GQA_KV_HEADS), tok(3 * D_MODEL)),
        compiler_params=_params(("arbitrary",)),
        name="in_proj",
    )(h, mod4, mod4, norm_w, cos, sin, qw, kw, seg_ones, w_main, w_dt_hi, w_dt_lo)
    return outs


def _softmax_pv(s_list, v_list):
    m = s_list[0].max(axis=-1, keepdims=True)
    for s in s_list[1:]:
        m = jnp.maximum(m, s.max(axis=-1, keepdims=True))
    den = 0.0
    out = 0.0
    for s, v in zip(s_list, v_list):
        p = jnp.exp(s - m)
        den = den + p.sum(axis=-1, keepdims=True)
        out = out + _mm(p.astype(BF16), v)
    return out / den


def _na_kernel(q_ref, k_ref, v_ref, kc_ref, vc_ref, bm_ref, o_ref, *, n_rows):
    n_groups = n_rows // NA_ROW_GROUP
    gq = NA_ROW_GROUP * GRID_W
    slab = NA_SLAB_ROWS * GRID_W
    kc = kc_ref[...]
    vc = vc_ref[...]
    head0 = lax.broadcasted_iota(jnp.int32, (1, LANES), 1) < HEAD_DIM

    def group(i, carry):
        r0 = i * NA_ROW_GROUP
        start = jnp.clip(r0 - NA_WIN_ROWS // 2, 0, n_rows - NA_SLAB_ROWS)
        cls = jnp.where(i == 0, 0, jnp.where(i == n_groups - 1, 2, 1))
        q2 = q_ref[pl.ds(pl.multiple_of(r0 * GRID_W, gq), gq), :]
        ks = k_ref[pl.ds(pl.multiple_of(start * GRID_W, GRID_W), slab), :]
        vs = v_ref[pl.ds(pl.multiple_of(start * GRID_W, GRID_W), slab), :]
        zero = jnp.zeros_like(q2)
        qs = jnp.concatenate([jnp.where(head0, q2, zero), jnp.where(head0, zero, q2)], axis=0)
        s_win = _mm_nt(qs, ks) + bm_ref[cls]
        s_ctx = _mm_nt(qs, kc)
        o = _softmax_pv([s_win, s_ctx], [vs, vc])
        o_ref[pl.ds(pl.multiple_of(r0 * GRID_W, gq), gq), :] = jnp.where(head0, o[:gq], o[gq:]).astype(BF16)
        return carry

    lax.fori_loop(0, n_groups, group, 0)


def _neighbourhood_attention(na_l, na_c, bias_mask, n_batch, seq_len, ctx_len):
    n_rows = seq_len // GRID_W
    n_pairs = NA_W // LANES
    lat = lambda part: pl.BlockSpec((seq_len, LANES), lambda b, p: (b, part * n_pairs + p))
    ctx = lambda part: pl.BlockSpec((ctx_len, LANES), lambda b, p: (b, part * n_pairs + p))
    return pl.pallas_call(
        functools.partial(_na_kernel, n_rows=n_rows),
        out_shape=jax.ShapeDtypeStruct((n_batch * seq_len, NA_W), BF16),
        grid=(n_batch, n_pairs),
        in_specs=[lat(0), lat(1), lat(2), ctx(1), ctx(2),
                  pl.BlockSpec((None, 3, 2 * NA_ROW_GROUP * GRID_W, NA_SLAB_ROWS * GRID_W),
                               lambda b, p: (p, 0, 0, 0))],
        out_specs=pl.BlockSpec((seq_len, LANES), lambda b, p: (b, p)),
        compiler_params=_params(("arbitrary", "arbitrary")),
        name="neighbourhood_attention",
    )(na_l, na_l, na_l, na_c, na_c, bias_mask)


def _ctx_mha_kernel(q_ref, k_ref, v_ref, o_ref):
    q2 = q_ref[...]
    k2 = k_ref[...]
    v2 = v_ref[...]
    head0 = lax.broadcasted_iota(jnp.int32, (1, LANES), 1) < HEAD_DIM
    outs = []
    for hh in range(2):
        keep = head0 if hh == 0 else jnp.logical_not(head0)
        qm = jnp.where(keep, q2, jnp.zeros_like(q2))
        outs.append(_softmax_pv([_mm_nt(qm, k2)], [v2]))
    o_ref[...] = jnp.where(head0, outs[0], outs[1]).astype(BF16)


def _context_mha(na_c, n_batch, ctx_len):
    n_pairs = NA_W // LANES
    part = lambda which: pl.BlockSpec((ctx_len, LANES), lambda b, p: (b, which * n_pairs + p))
    return pl.pallas_call(
        _ctx_mha_kernel,
        out_shape=jax.ShapeDtypeStruct((n_batch * ctx_len, NA_W), BF16),
        grid=(n_batch, n_pairs),
        in_specs=[part(0), part(1), part(2)],
        out_specs=pl.BlockSpec((ctx_len, LANES), lambda b, p: (b, p)),
        compiler_params=_params(("arbitrary", "arbitrary")),
        name="context_mha",
    )(na_c, na_c, na_c)


def _gqa_kernel(q_ref, k_ref, v_ref, o_ref, s_sc, p_sc, m_sc, a_sc, acc_sc, *, tq, n_blocks):
    rows = GQA_GROUP * tq
    q = q_ref[...].reshape(rows, HEAD_DIM)
    m_sc[...] = jnp.full_like(m_sc, NEG_BIG)
    acc_sc[...] = jnp.zeros_like(acc_sc)

    def scores(blk, slot):
        s_sc[slot] = _mm_nt(q, k_ref[blk * GQA_BK:(blk + 1) * GQA_BK, :])

    def absorb(blk, slot):
        for c0 in range(0, rows, GQA_ROW_CHUNK):
            rs = slice(c0, c0 + GQA_ROW_CHUNK)
            tiles = [s_sc[slot, rs, t * LANES:(t + 1) * LANES] for t in range(GQA_BK // LANES)]
            m_old = m_sc[rs, :]
            m_new = jnp.maximum(m_old, functools.reduce(jnp.maximum, tiles).max(axis=-1, keepdims=True))
            a_sc[rs, :] = jnp.exp(m_old - m_new)
            m_sc[rs, :] = m_new
            for t, s in enumerate(tiles):
                p_sc[rs, t * LANES:(t + 1) * LANES] = jnp.exp(s - m_new).astype(BF16)
        acc_sc[...] = a_sc[...] * acc_sc[...] + _mm(p_sc[...], v_ref[blk * GQA_BK:(blk + 1) * GQA_BK, :])

    scores(0, 0)
    for j in range(n_blocks):
        if j + 1 < n_blocks:
            scores(j + 1, (j + 1) % 2)
        absorb(j, j % 2)
    acc = acc_sc[...]
    o = acc[:, :HEAD_DIM] / acc[:, HEAD_DIM:HEAD_DIM + 1]
    o_ref[...] = jnp.concatenate([o[g * tq:(g + 1) * tq] for g in range(GQA_GROUP)], axis=-1).astype(BF16)


def _gqa_attention(q_hm, k_all, v_aug, n_batch, q_len, tq):
    q_tiles = q_len // tq
    nk = k_all.shape[2]
    n_blocks = nk // GQA_BK
    assert nk % GQA_BK == 0
    rows = GQA_GROUP * tq
    return pl.pallas_call(
        functools.partial(_gqa_kernel, tq=tq, n_blocks=n_blocks),
        out_shape=jax.ShapeDtypeStruct((n_batch * q_len, GQA_Q), BF16),
        grid=(n_batch, GQA_KV_HEADS, q_tiles),
        in_specs=[
            pl.BlockSpec((GQA_GROUP, tq, HEAD_DIM), lambda b, g, i: (g, b * q_tiles + i, 0)),
            pl.BlockSpec((None, None, nk, HEAD_DIM), lambda b, g, i: (g, b, 0, 0)),
            pl.BlockSpec((None, None, nk, LANES), lambda b, g, i: (g, b, 0, 0)),
        ],
        out_specs=pl.BlockSpec((tq, GQA_GROUP * HEAD_DIM), lambda b, g, i: (b * q_tiles + i, g)),
        scratch_shapes=[
            pltpu.VMEM((2, rows, GQA_BK), F32),
            pltpu.VMEM((rows, GQA_BK), BF16),
            pltpu.VMEM((rows, LANES), F32),
            pltpu.VMEM((rows, LANES), F32),
            pltpu.VMEM((rows, LANES), F32),
        ],
        compiler_params=_params(("arbitrary", "arbitrary", "arbitrary")),
        name="gqa_attention",
    )(q_hm, k_all, v_aug)


def _ssd_kernel(xl_ref, xlp_ref, xln_ref, xc_ref, xcp_ref, xcn_ref, dtl_ref, dtc_ref, zl_ref, zc_ref,
                cw_ref, cb_ref, alog_ref, dtb_ref, dsk_ref, nw_ref, tri_ref, shift_ref,
                outl_ref, outc_ref, x_sc, dt_sc, z_sc, y_sc, st_sc, *, n_chunks, n_ctx_chunks):
    T = SSM_CHUNK
    s = pl.program_id(1)
    direction = s // n_chunks
    sp = s % n_chunks
    gid = jnp.where(direction == 0, sp,
                    jnp.where(sp < n_ctx_chunks, n_ctx_chunks - 1 - sp, n_chunks + n_ctx_chunks - 1 - sp))
    is_ctx = gid < n_ctx_chunks
    seg_first = jnp.logical_or(gid == 0, gid == n_ctx_chunks)
    seg_last = jnp.logical_or(gid == n_ctx_chunks - 1, gid == n_chunks - 1)

    def stage(x_ref, xp_ref, xn_ref, dt_ref, z_ref):
        x_sc[0:BF16_ROWS] = jnp.where(seg_first, jnp.zeros_like(xp_ref[...]), xp_ref[...])
        x_sc[BF16_ROWS:BF16_ROWS + T] = x_ref[...]
        x_sc[BF16_ROWS + T:] = jnp.where(seg_last, jnp.zeros_like(xn_ref[...]), xn_ref[...])
        dt_sc[...] = dt_ref[...]
        z_sc[...] = z_ref[...]

    @pl.when(is_ctx)
    def _():
        stage(xc_ref, xcp_ref, xcn_ref, dtc_ref, zc_ref)

    @pl.when(jnp.logical_not(is_ctx))
    def _():
        stage(xl_ref, xlp_ref, xln_ref, dtl_ref, zl_ref)

    @pl.when(sp == 0)
    def _():
        st_sc[...] = jnp.zeros_like(st_sc)

    ext = x_sc[...]
    x_mid = ext[BF16_ROWS:BF16_ROWS + T].astype(F32)
    x_dn = _mm(shift_ref[0], ext)
    x_up = _mm(shift_ref[1], ext)
    xbc = _silu(cw_ref[0:1] * x_dn + cw_ref[1:2] * x_mid + cw_ref[2:3] * x_up + cb_ref[...])
    xs = xbc[:, :SSM_D_INNER]

    dt = _softplus(dt_sc[...] + dtb_ref[...])
    da = dt * (-jnp.exp(alog_ref[...]))
    tri = tri_ref[direction]
    tri_t = tri_ref[1 - direction]
    da_hi, da_lo = _split_bf16(da)
    cs = _mm(tri, da_hi) + _mm(tri, da_lo)
    da_t = da.T
    dat_hi, dat_lo = _split_bf16(da_t)
    cs_t = _mm(dat_hi, tri_t) + _mm(dat_lo, tri_t)
    tot_t = da_t.sum(axis=-1, keepdims=True)
    dt_t = dt.T
    w_t = dt_t * jnp.exp(tot_t - cs_t)
    causal = tri > 0

    ys = []
    for g in range(SSM_GROUPS):
        b_g = xbc[:, SSM_D_INNER + g * SSM_D_STATE:SSM_D_INNER + (g + 1) * SSM_D_STATE]
        c_g = xbc[:, SSM_D_INNER + SSM_BC + g * SSM_D_STATE:SSM_D_INNER + SSM_BC + (g + 1) * SSM_D_STATE]
        cb = _mm_nt(c_g.astype(BF16), b_g.astype(BF16))
        b_gt = b_g.T
        for r in range(SSM_HEADS // SSM_GROUPS):
            h = g * (SSM_HEADS // SSM_GROUPS) + r
            x_h = xs[:, h * HEAD_DIM:(h + 1) * HEAD_DIM].astype(BF16)
            col = cs[:, h:h + 1]
            decay = jnp.exp(jnp.where(causal, col - cs_t[h:h + 1, :], NEG_BIG))
            m_h = (cb * decay * dt_t[h:h + 1, :]).astype(BF16)
            st = st_sc[h]
            y_h = _mm(m_h, x_h) + _mm((c_g * jnp.exp(col)).astype(BF16), st.astype(BF16))
            ys.append(y_h)
            st_sc[h] = jnp.exp(tot_t[h:h + 1, :]) * st + _mm((b_gt * w_t[h:h + 1, :]).astype(BF16), x_h)
    y = jnp.concatenate(ys, axis=-1)

    @pl.when(direction == 0)
    def _():
        y_sc[gid] = y + dsk_ref[...] * xs

    @pl.when(direction == 1)
    def _():
        gated = (y_sc[gid] + y) * _silu(z_sc[...].astype(F32))
        res = (_rms(gated) * nw_ref[...]).astype(BF16)

        @pl.when(is_ctx)
        def _():
            outc_ref[...] = res

        @pl.when(jnp.logical_not(is_ctx))
        def _():
            outl_ref[...] = res


def _ssd_mixer(xbc_l, dt_l, z_l, xbc_c, dt_c, z_c, conv_w8, conv_b, a_log_p, dt_bias_p, d_skip_x, norm_w,
               tri, shift, n_batch, seq_len, ctx_len):
    T = SSM_CHUNK
    nlc = seq_len // T
    ncc = ctx_len // T
    n_chunks = nlc + ncc
    halo = T // BF16_ROWS

    def ids(s):
        direction = s // n_chunks
        sp = s % n_chunks
        gid = jnp.where(direction == 0, sp, jnp.where(sp < ncc, ncc - 1 - sp, n_chunks + ncc - 1 - sp))
        return direction, gid, jnp.clip(gid, 0, ncc - 1), jnp.clip(gid - ncc, 0, nlc - 1)

    def lat_main(b, s):
        return (b * nlc + ids(s)[3], 0)

    def lat_prev(b, s):
        return (jnp.maximum((b * nlc + ids(s)[3]) * halo - 1, 0), 0)

    def lat_next(b, s):
        return (jnp.minimum((b * nlc + ids(s)[3] + 1) * halo, n_batch * nlc * halo - 1), 0)

    def ctx_main(b, s):
        return (b * ncc + ids(s)[2], 0)

    def ctx_prev(b, s):
        return (jnp.maximum((b * ncc + ids(s)[2]) * halo - 1, 0), 0)

    def ctx_next(b, s):
        return (jnp.minimum((b * ncc + ids(s)[2] + 1) * halo, n_batch * ncc * halo - 1), 0)

    def lat_dt(b, s):
        return (b * nlc + ids(s)[3], ids(s)[0])

    def ctx_dt(b, s):
        return (b * ncc + ids(s)[2], ids(s)[0])

    def lat_out(b, s):
        direction, gid, _, lc = ids(s)
        return (b * nlc + jnp.where(jnp.logical_and(direction == 1, gid >= ncc), lc, nlc - 1), 0)

    def ctx_out(b, s):
        direction, gid, cc, _ = ids(s)
        return (b * ncc + jnp.where(direction == 1, jnp.where(gid < ncc, cc, 0), ncc - 1), 0)

    def by_dir(b, s):
        return (ids(s)[0], 0, 0)

    const2 = lambda b, s: (0, 0)
    const3 = lambda b, s: (0, 0, 0)
    return pl.pallas_call(
        functools.partial(_ssd_kernel, n_chunks=n_chunks, n_ctx_chunks=ncc),
        out_shape=(jax.ShapeDtypeStruct((n_batch * seq_len, SSM_D_INNER), BF16),
                   jax.ShapeDtypeStruct((n_batch * ctx_len, SSM_D_INNER), BF16)),
        grid=(n_batch, 2 * n_chunks),
        in_specs=[
            pl.BlockSpec((T, SSM_XBC), lat_main),
            pl.BlockSpec((BF16_ROWS, SSM_XBC), lat_prev),
            pl.BlockSpec((BF16_ROWS, SSM_XBC), lat_next),
            pl.BlockSpec((T, SSM_XBC), ctx_main),
            pl.BlockSpec((BF16_ROWS, SSM_XBC), ctx_prev),
            pl.BlockSpec((BF16_ROWS, SSM_XBC), ctx_next),
            pl.BlockSpec((T, LANES), lat_dt),
            pl.BlockSpec((T, LANES), ctx_dt),
            pl.BlockSpec((T, SSM_D_INNER), lat_out),
            pl.BlockSpec((T, SSM_D_INNER), ctx_out),
            pl.BlockSpec((F32_ROWS, SSM_XBC), const2),
            pl.BlockSpec((1, SSM_XBC), const2),
            pl.BlockSpec((None, 1, LANES), by_dir),
            pl.BlockSpec((None, 1, LANES), by_dir),
            pl.BlockSpec((1, SSM_D_INNER), const2),
            pl.BlockSpec((1, SSM_D_INNER), const2),
            pl.BlockSpec((2, T, T), const3),
            pl.BlockSpec((2, T, T + 2 * BF16_ROWS), const3),
        ],
        out_specs=(pl.BlockSpec((T, SSM_D_INNER), lat_out), pl.BlockSpec((T, SSM_D_INNER), ctx_out)),
        scratch_shapes=[
            pltpu.VMEM((T + 2 * BF16_ROWS, SSM_XBC), BF16),
            pltpu.VMEM((T, LANES), F32),
            pltpu.VMEM((T, SSM_D_INNER), BF16),
            pltpu.VMEM((n_chunks, T, SSM_D_INNER), F32),
            pltpu.VMEM((SSM_HEADS, SSM_D_STATE, HEAD_DIM), F32),
        ],
        compiler_params=_params(("arbitrary", "arbitrary")),
        name="ssd_mixer",
    )(xbc_l, xbc_l, xbc_l, xbc_c, xbc_c, xbc_c, dt_l, dt_c, z_l, z_c,
      conv_w8, conv_b, a_log_p, dt_bias_p, d_skip_x, norm_w, tri, shift)


def _merge_kernel(h_ref, na_ref, ssm_ref, gqa_ref, gate_ref, g_ref, w1_ref, w2_ref, w3_ref, wo_ref, o_ref):
    gates = gate_ref[...]
    t = _sigmoid(gates[:, :D_MODEL].astype(F32)) * _mm(na_ref[...], w1_ref[...])
    t = t + _sigmoid(gates[:, D_MODEL:2 * D_MODEL].astype(F32)) * _mm(ssm_ref[...], w2_ref[...])
    t = t + _sigmoid(gates[:, 2 * D_MODEL:].astype(F32)) * _mm(gqa_ref[...], w3_ref[...])
    o_ref[...] = h_ref[...] + g_ref[...] * _mm(t.astype(BF16), wo_ref[...])


def _merge(h, o_na, o_ssm, o_gqa, gates, mod4, row_fn, w1, w2, w3, wo, tm):
    n_tok = h.shape[0]
    tok = lambda width: pl.BlockSpec((tm, width), lambda i: (i, 0))
    return pl.pallas_call(
        _merge_kernel,
        out_shape=jax.ShapeDtypeStruct((n_tok, D_MODEL), F32),
        grid=(n_tok // tm,),
        in_specs=[tok(D_MODEL), tok(NA_W), tok(SSM_D_INNER), tok(GQA_Q), tok(3 * D_MODEL),
                  _mod_spec(2, row_fn),
                  _resident((NA_W, D_MODEL)), _resident((SSM_D_INNER, D_MODEL)),
                  _resident((GQA_Q, D_MODEL)), _resident((D_MODEL, D_MODEL))],
        out_specs=tok(D_MODEL),
        compiler_params=_params(("arbitrary",)),
        name="branch_merge",
    )(h, o_na, o_ssm, o_gqa, gates, mod4, w1, w2, w3, wo)


def _ffn_kernel(h_ref, hp_ref, hn_ref, sh_ref, sc_ref, g_ref, nw_ref, wup_ref, cw_ref, wdn_ref, fw_ref,
                o_ref, hn_sc, acc_sc, *, tm, tiles_per_seq, final_norm):
    j = pl.program_id(0) % tiles_per_seq
    seq_first = j == 0
    seq_last = j == tiles_per_seq - 1

    def norm_mod(x):
        return (_rms(x) * nw_ref[...]) * (1.0 + sc_ref[...]) + sh_ref[...]

    x = h_ref[...]
    hn_sc[0:tm] = norm_mod(x).astype(BF16)
    before = jnp.where(seq_first, 0.0, norm_mod(hp_ref[...]))
    after = jnp.where(seq_last, 0.0, norm_mod(hn_ref[...]))
    hn_sc[tm:] = jnp.concatenate([before, after], axis=0).astype(BF16)
    acc_sc[...] = jnp.zeros_like(acc_sc)
    row = lax.broadcasted_iota(jnp.int32, (tm, 1), 0)

    def chunk(c, carry):
        u = _mm(hn_sc[...], wup_ref[c])
        u_mid = u[0:tm]
        u_before = u[tm + F32_ROWS - 1:tm + F32_ROWS]
        u_after = u[tm + F32_ROWS:tm + F32_ROWS + 1]
        u_dn = jnp.where(row == 0, u_before, pltpu.roll(u_mid, 1, 0))
        u_up = jnp.where(row == tm - 1, u_after, pltpu.roll(u_mid, tm - 1, 0))
        cw = cw_ref[c]
        y = cw[0:1] * u_dn + cw[1:2] * u_mid + cw[2:3] * u_up + cw[3:4]
        act = (_silu(y[:, :FFN_TN]) * y[:, FFN_TN:]).astype(BF16)
        acc_sc[...] += _mm(act, wdn_ref[c])
        return carry

    lax.fori_loop(0, FFN_CHUNKS, chunk, 0, unroll=True)
    out = x + g_ref[...] * acc_sc[...]
    if final_norm:
        out = _rms(out) * fw_ref[...]
    o_ref[...] = out


def _conv_ffn(h, mod4, row_fn, norm_w, w_up_c, conv_c, w_down_c, final_w, tm, seq_len, final_norm):
    n_tok = h.shape[0]
    halo = tm // F32_ROWS
    n_halo = n_tok // F32_ROWS
    return pl.pallas_call(
        functools.partial(_ffn_kernel, tm=tm, tiles_per_seq=seq_len // tm, final_norm=final_norm),
        out_shape=jax.ShapeDtypeStruct((n_tok, D_MODEL), F32),
        grid=(n_tok // tm,),
        in_specs=[
            pl.BlockSpec((tm, D_MODEL), lambda i: (i, 0)),
            pl.BlockSpec((F32_ROWS, D_MODEL), lambda i: (jnp.maximum(i * halo - 1, 0), 0)),
            pl.BlockSpec((F32_ROWS, D_MODEL), lambda i: (jnp.minimum((i + 1) * halo, n_halo - 1), 0)),
            _mod_spec(3, row_fn),
            _mod_spec(4, row_fn),
            _mod_spec(5, row_fn),
            _resident((1, D_MODEL)),
            _resident((FFN_CHUNKS, D_MODEL, 2 * FFN_TN)),
            _resident((FFN_CHUNKS, F32_ROWS, 2 * FFN_TN)),
            _resident((FFN_CHUNKS, FFN_TN, D_MODEL)),
            _resident((1, D_MODEL)),
        ],
        out_specs=pl.BlockSpec((tm, D_MODEL), lambda i: (i, 0)),
        scratch_shapes=[pltpu.VMEM((tm + 2 * F32_ROWS, D_MODEL), BF16), pltpu.VMEM((tm, D_MODEL), F32)],
        compiler_params=_params(("arbitrary",)),
        name="conv_ffn",
    )(h, h, h, mod4, mod4, mod4, norm_w, w_up_c, conv_c, w_down_c, final_w)


def _rope_tables(seq_len):
    n_freq = HEAD_DIM // 4
    inv_freq = ROPE_THETA ** (-jnp.arange(n_freq, dtype=F32) / n_freq)
    t = jnp.arange(seq_len, dtype=jnp.int32)
    row = (t // GRID_W).astype(F32)
    col = (t % GRID_W).astype(F32)
    ang = jnp.concatenate([row[:, None] * inv_freq, col[:, None] * inv_freq], axis=-1)
    cos = jnp.repeat(jnp.cos(ang), 2, axis=-1)
    sin = jnp.repeat(jnp.sin(ang), 2, axis=-1) * jnp.tile(jnp.asarray([-1.0, 1.0], F32), HEAD_DIM // 2)
    return jnp.tile(cos, (1, LANES // HEAD_DIM)), jnp.tile(sin, (1, LANES // HEAD_DIM))


def _na_bias_mask(rel_bias):
    qc = np.arange(GRID_W)[:, None]
    kc = np.arange(GRID_W)[None, :]
    win0 = np.clip(qc - NA_WIN_COLS // 2, 0, GRID_W - NA_WIN_COLS)
    ok = (kc >= win0) & (kc < win0 + NA_WIN_COLS)
    dcol = np.clip(kc - qc + NA_WIN_COLS - 1, 0, 2 * NA_WIN_COLS - 2)
    per_drow = jnp.where(jnp.asarray(ok)[None, None], rel_bias.astype(F32)[:, :, dcol], NEG_BIG)
    n_heads = rel_bias.shape[0]
    masked = jnp.full((n_heads, GRID_W, GRID_W), NEG_BIG, F32)
    half = NA_WIN_ROWS // 2
    classes = []
    for cls in range(3):
        rows = []
        for j in range(NA_ROW_GROUP):
            first_slab_row = (0, j, NA_SLAB_ROWS - NA_WIN_ROWS)[cls]
            drow0 = (NA_WIN_ROWS - 1 - j, NA_WIN_ROWS - 1 - half, NA_WIN_ROWS - 1 - half - j)[cls]
            blocks = [per_drow[:, drow0 + sr - first_slab_row] if 0 <= sr - first_slab_row < NA_WIN_ROWS else masked
                      for sr in range(NA_SLAB_ROWS)]
            rows.append(jnp.concatenate(blocks, axis=-1))
        classes.append(jnp.concatenate(rows, axis=1))
    table = jnp.stack(classes, axis=1)
    table = table.reshape(n_heads // 2, 2, 3, NA_ROW_GROUP * GRID_W, NA_SLAB_ROWS * GRID_W)
    return jnp.moveaxis(table, 1, 2).reshape(n_heads // 2, 3, 2 * NA_ROW_GROUP * GRID_W, NA_SLAB_ROWS * GRID_W)


def _pack_w_in(w_in):
    sizes = [3 * NA_W, SSM_D_INNER, SSM_XBC, 2 * SSM_HEADS, GQA_Q, GQA_KV, GQA_KV]
    na, z, xbc, dt, gq, gk, gv, gate = jnp.split(w_in, [int(i) for i in np.cumsum(sizes)], axis=-1)
    na = jnp.concatenate([na[:, :NA_W] * HEAD_DIM ** -0.5, na[:, NA_W:]], axis=-1)
    w_main = jnp.concatenate([na, z, xbc, gq, gk, gv, gate], axis=-1).astype(BF16)
    dt_pad = jnp.zeros((D_MODEL, DT_COLS), F32)
    dt_pad = dt_pad.at[:, :SSM_HEADS].set(dt[:, :SSM_HEADS]).at[:, LANES:LANES + SSM_HEADS].set(dt[:, SSM_HEADS:])
    dt_hi, dt_lo = _split_bf16(dt_pad)
    return w_main, dt_hi, dt_lo


def _pad_heads(v):
    return jnp.zeros((2, 1, LANES), F32).at[:, 0, :SSM_HEADS].set(v.astype(F32))


def _pack_ffn(w_up, conv_w, conv_b, w_down):
    def chunked(t):
        a, b = t[..., :FFN_HIDDEN], t[..., FFN_HIDDEN:]
        lead = t.shape[:-1]
        return jnp.concatenate([a.reshape(lead + (FFN_CHUNKS, FFN_TN)), b.reshape(lead + (FFN_CHUNKS, FFN_TN))], axis=-1)

    w_up_c = jnp.moveaxis(chunked(w_up), 1, 0).astype(BF16)
    taps = jnp.concatenate([conv_w, conv_b[None], jnp.zeros((F32_ROWS - 4, 2 * FFN_HIDDEN), F32)], axis=0)
    conv_c = jnp.moveaxis(chunked(taps), 1, 0)
    w_down_c = w_down.reshape(FFN_CHUNKS, FFN_TN, D_MODEL).astype(BF16)
    return w_up_c, conv_c, w_down_c


def _scan_constants():
    T = SSM_CHUNK
    t = np.arange(T)
    tril = (t[:, None] >= t[None, :]).astype(np.float32)
    tri = jnp.asarray(np.stack([tril, tril.T]), BF16)
    e = np.arange(T + 2 * BF16_ROWS)
    down = (e[None, :] == t[:, None] + BF16_ROWS - 1).astype(np.float32)
    up = (e[None, :] == t[:, None] + BF16_ROWS + 1).astype(np.float32)
    return tri, jnp.asarray(np.stack([down, up]), BF16)


def kernel(x, c, ctx, c_ctx, w_mod, b_mod, norm1_w, norm2_w, w_in, na_rel_bias, ssm_conv_w, ssm_conv_b, ssm_a_log, ssm_dt_bias, ssm_d, ssm_norm_w, q_norm_w, k_norm_w, w_out_na, w_out_ssm, w_out_gqa, w_o, ffn_w_up, ffn_conv_w, ffn_conv_b, ffn_w_down, final_norm_w):
    n_batch, seq_len, _ = x.shape
    ctx_len = ctx.shape[1]
    depth = w_mod.shape[0]
    assert seq_len % (GRID_W * NA_ROW_GROUP) == 0 and seq_len // GRID_W >= NA_SLAB_ROWS
    assert n_batch <= CTX_MOD_ROW and seq_len % 512 == 0 and ctx_len % 256 == 0
    tm_lat, tm_ctx = 512, 256

    cc = jnp.zeros((MOD_ROWS, D_MODEL), F32).at[:n_batch].set(c).at[CTX_MOD_ROW].set(c_ctx)
    mods = _mod_vectors(cc, w_mod, b_mod)
    lat_row = lambda i: i // (seq_len // tm_lat)
    ctx_row = lambda i: CTX_MOD_ROW

    cos_l, sin_l = _rope_tables(seq_len)
    cos_c, sin_c = jnp.ones((ctx_len, LANES), F32), jnp.zeros((ctx_len, LANES), F32)
    head_of_lane = np.arange(LANES) // HEAD_DIM
    seg_ones = jnp.asarray(head_of_lane[:, None] == head_of_lane[None, :], BF16)
    tri, shift = _scan_constants()

    h_lat = x.reshape(n_batch * seq_len, D_MODEL)
    h_ctx = ctx.reshape(n_batch * ctx_len, D_MODEL)
    row = lambda v: v.reshape(1, -1).astype(F32)
    for layer in range(depth):
        need_ctx = layer < depth - 1
        last = layer == depth - 1
        mod4 = mods[layer].reshape(MOD_ROWS, N_MOD, 1, D_MODEL)
        w_main, w_dt_hi, w_dt_lo = _pack_w_in(w_in[layer])
        qw = row(jnp.tile(q_norm_w[layer], LANES // HEAD_DIM))
        kw = row(jnp.tile(k_norm_w[layer], LANES // HEAD_DIM))
        proj = functools.partial(_in_proj, mod4=mod4, norm_w=row(norm1_w[layer]), qw=qw, kw=kw, seg_ones=seg_ones,
                                 w_main=w_main, w_dt_hi=w_dt_hi, w_dt_lo=w_dt_lo)
        na_l, z_l, xbc_l, dt_l, q_l, k_l, v_l, gate_l = proj(
            h_lat, row_fn=lat_row, cos=cos_l, sin=sin_l, tm=tm_lat, seq_len=seq_len)
        na_c, z_c, xbc_c, dt_c, q_c, k_c, v_c, gate_c = proj(
            h_ctx, row_fn=ctx_row, cos=cos_c, sin=sin_c, tm=tm_ctx, seq_len=ctx_len)

        o_na_l = _neighbourhood_attention(na_l, na_c, _na_bias_mask(na_rel_bias[layer]), n_batch, seq_len, ctx_len)
        conv_w8 = jnp.concatenate([ssm_conv_w[layer], jnp.zeros((F32_ROWS - 3, SSM_XBC), F32)], axis=0)
        o_ssm_l, o_ssm_c = _ssd_mixer(
            xbc_l, dt_l, z_l, xbc_c, dt_c, z_c, conv_w8, row(ssm_conv_b[layer]), _pad_heads(ssm_a_log[layer]),
            _pad_heads(ssm_dt_bias[layer]), row(jnp.repeat(ssm_d[layer], HEAD_DIM)), row(ssm_norm_w[layer]),
            tri, shift, n_batch, seq_len, ctx_len)
        per_batch = lambda t, n: t.reshape(GQA_KV_HEADS, n_batch, n, HEAD_DIM)
        with_ones = lambda v: jnp.concatenate(
            [v, jnp.ones(v.shape[:-1] + (1,), BF16), jnp.zeros(v.shape[:-1] + (LANES - HEAD_DIM - 1,), BF16)], axis=-1)
        k_ctx, v_ctx = per_batch(k_c, ctx_len), with_ones(per_batch(v_c, ctx_len))
        k_all = jnp.concatenate([k_ctx, per_batch(k_l, seq_len)], axis=2)
        v_all = jnp.concatenate([v_ctx, with_ones(per_batch(v_l, seq_len))], axis=2)
        o_gqa_l = _gqa_attention(q_l, k_all, v_all, n_batch, seq_len, tq=256)

        w1, w2, w3, wo = (w.astype(BF16) for w in (w_out_na[layer], w_out_ssm[layer], w_out_gqa[layer], w_o[layer]))
        ffn_w = _pack_ffn(ffn_w_up[layer], ffn_conv_w[layer], ffn_conv_b[layer], ffn_w_down[layer])
        h_lat = _merge(h_lat, o_na_l, o_ssm_l, o_gqa_l, gate_l, mod4, lat_row, w1, w2, w3, wo, tm_lat)
        h_lat = _conv_ffn(h_lat, mod4, lat_row, row(norm2_w[layer]), *ffn_w, row(final_norm_w),
                          tm=tm_lat, seq_len=seq_len, final_norm=last)
        if need_ctx:
            o_na_c = _context_mha(na_c, n_batch, ctx_len)
            o_gqa_c = _gqa_attention(q_c, k_ctx, v_ctx, n_batch, ctx_len, tq=ctx_len)
            h_ctx = _merge(h_ctx, o_na_c, o_ssm_c, o_gqa_c, gate_c, mod4, ctx_row, w1, w2, w3, wo, tm_ctx)
            h_ctx = _conv_ffn(h_ctx, mod4, ctx_row, row(norm2_w[layer]), *ffn_w, row(final_norm_w),
                              tm=tm_ctx, seq_len=ctx_len, final_norm=False)
    return h_lat.reshape(n_batch, seq_len, D_MODEL)
```

```python
import functools

import numpy as np
import jax
import jax.numpy as jnp
from jax import lax
from jax.experimental import pallas as pl
from jax.experimental.pallas import tpu as pltpu

F32 = jnp.float32
BF16 = jnp.bfloat16

D_MODEL = 1024
GRID_W = 64
EPS = 1e-6
HEAD_DIM = 64
NA_HEADS = 8
NA_W = NA_HEADS * HEAD_DIM
NA_WIN_ROWS = 8
NA_WIN_COLS = 16
NA_ROW_GROUP = 4
NA_SLAB_ROWS = 12
SSM_HEADS = 16
SSM_D_INNER = SSM_HEADS * HEAD_DIM
SSM_GROUPS = 2
SSM_D_STATE = 64
SSM_BC = SSM_GROUPS * SSM_D_STATE
SSM_XBC = SSM_D_INNER + 2 * SSM_BC
SSM_CHUNK = 128
SSD_SEQS = 2
GQA_HEADS = 8
GQA_KV_HEADS = 2
GQA_GROUP = GQA_HEADS // GQA_KV_HEADS
GQA_Q = GQA_HEADS * HEAD_DIM
GQA_KV = GQA_KV_HEADS * HEAD_DIM
GQA_BK = 256
GQA_ROW_CHUNK = 128
ROPE_THETA = 10000.0
FFN_HIDDEN = 2816
N_MOD = 6
MOD_ROWS = 16
CTX_MOD_ROW = 8
NEG_BIG = -1e30

LANES = 128
BF16_ROWS = 16
F32_ROWS = 8
VMEM_LIMIT = 56 * 1024 * 1024

SEG_NA = (0, 3 * NA_W)
SEG_Z = (SEG_NA[1], SEG_NA[1] + SSM_D_INNER)
SEG_XBC = (SEG_Z[1], SEG_Z[1] + SSM_XBC)
SEG_GQ = (SEG_XBC[1], SEG_XBC[1] + GQA_Q)
SEG_GK = (SEG_GQ[1], SEG_GQ[1] + GQA_KV)
SEG_GV = (SEG_GK[1], SEG_GK[1] + GQA_KV)
SEG_GATE = (SEG_GV[1], SEG_GV[1] + 3 * D_MODEL)
W_MAIN_COLS = SEG_GATE[1]
DT_COLS = 2 * LANES
DOT_COLS = 512

FFN_TN = 256
FFN_CHUNKS = FFN_HIDDEN // FFN_TN


def _mm(a, b):
    return jnp.dot(a, b, preferred_element_type=F32)


def _mm_nt(a, b):
    return lax.dot_general(a, b, (((1,), (1,)), ((), ())), preferred_element_type=F32)


def _split_bf16(x):
    hi = x.astype(BF16)
    lo = (x - hi.astype(F32)).astype(BF16)
    return hi, lo


def _sigmoid(x):
    return 1.0 / (1.0 + jnp.exp(-x))


def _silu(x):
    return x * _sigmoid(x)


def _softplus(x):
    return jnp.maximum(x, 0.0) + jnp.log1p(jnp.exp(-jnp.abs(x)))


def _rms(x):
    return x * lax.rsqrt(jnp.mean(x * x, axis=-1, keepdims=True) + EPS)


def _resident(shape):
    nd = len(shape)
    return pl.BlockSpec(shape, lambda *_: (0,) * nd, pipeline_mode=pl.Buffered(1))


def _params(sem):
    return pltpu.CompilerParams(dimension_semantics=sem, vmem_limit_bytes=VMEM_LIMIT)


def _mod_kernel(c_ref, w_ref, b_ref, o_ref):
    x_hi, x_lo = _split_bf16(_silu(c_ref[...]))
    w_hi, w_lo = _split_bf16(w_ref[...])
    o_ref[...] = _mm(x_hi, w_hi) + _mm(x_lo, w_hi) + _mm(x_hi, w_lo) + b_ref[...]


def _mod_vectors(cc, w_mod, b_mod):
    n_layers = w_mod.shape[0]
    tn = D_MODEL
    return pl.pallas_call(
        _mod_kernel,
        out_shape=jax.ShapeDtypeStruct((n_layers, MOD_ROWS, N_MOD * D_MODEL), F32),
        grid=(n_layers, N_MOD),
        in_specs=[
            pl.BlockSpec((MOD_ROWS, D_MODEL), lambda l, j: (0, 0)),
            pl.BlockSpec((None, D_MODEL, tn), lambda l, j: (l, 0, j)),
            pl.BlockSpec((None, 1, tn), lambda l, j: (l, 0, j)),
        ],
        out_specs=pl.BlockSpec((None, MOD_ROWS, tn), lambda l, j: (l, 0, j)),
        compiler_params=_params(("arbitrary", "arbitrary")),
        name="mod_vectors",
    )(cc, w_mod, b_mod.reshape(n_layers, 1, N_MOD * D_MODEL))


def _mod_spec(which, row_fn):
    return pl.BlockSpec((None, None, 1, D_MODEL), lambda i: (row_fn(i), which, 0, 0))


def _qk_norm_rope(x, w, cos, sin, seg_ones):
    ss = _mm(jnp.concatenate(_split_bf16(x * x), axis=1), seg_ones)
    xn = x * lax.rsqrt(ss * (1.0 / HEAD_DIM) + EPS) * w
    lane = lax.broadcasted_iota(jnp.int32, xn.shape, 1)
    partner = jnp.where((lane & 1) == 0, pltpu.roll(xn, LANES - 1, 1), pltpu.roll(xn, 1, 1))
    return xn * cos + partner * sin


def _conv3_rows(u, tm, taps):
    row = lax.broadcasted_iota(jnp.int32, (tm, 1), 0)
    u_mid = u[0:tm]
    u_before = u[tm + F32_ROWS - 1:tm + F32_ROWS]
    u_after = u[tm + F32_ROWS:tm + F32_ROWS + 1]
    u_dn = jnp.where(row == 0, u_before, pltpu.roll(u_mid, 1, 0))
    u_up = jnp.where(row == tm - 1, u_after, pltpu.roll(u_mid, tm - 1, 0))
    return taps[0:1] * u_dn + taps[1:2] * u_mid + taps[2:3] * u_up + taps[3:4]


def _inproj_kernel(h_ref, hp_ref, hn_ref, sh_ref, sc_ref, nw_ref, cos_ref, sin_ref, qw_ref, kw_ref, seg_ref,
                   cw_ref, dtb_ref, w_ref, wdh_ref, wdl_ref,
                   na_ref, z_ref, xbc_ref, dt_ref, q_ref, k_ref, v_ref, gate_ref, y_sc, *, tm, tiles_per_seq):
    j = pl.program_id(0) % tiles_per_seq

    def norm_mod(x):
        return (_rms(x) * nw_ref[...]) * (1.0 + sc_ref[...]) + sh_ref[...]

    y = norm_mod(h_ref[...])
    y_hi, y_lo = _split_bf16(y)
    before = jnp.where(j == 0, 0.0, norm_mod(hp_ref[...]))
    after = jnp.where(j == tiles_per_seq - 1, 0.0, norm_mod(hn_ref[...]))
    y_sc[0:tm] = y_hi
    y_sc[tm:] = jnp.concatenate([before, after], axis=0).astype(BF16)

    def project(out_ref, seg):
        for c0 in range(seg[0], seg[1], DOT_COLS):
            c1 = min(c0 + DOT_COLS, seg[1])
            out_ref[:, c0 - seg[0]:c1 - seg[0]] = _mm(y_hi, w_ref[:, c0:c1]).astype(out_ref.dtype)

    project(na_ref, SEG_NA)
    project(z_ref, SEG_Z)
    project(gate_ref, SEG_GATE)
    for c0 in range(SEG_XBC[0], SEG_XBC[1], DOT_COLS):
        c1 = min(c0 + DOT_COLS, SEG_XBC[1])
        cols = slice(c0 - SEG_XBC[0], c1 - SEG_XBC[0])
        xbc_ref[:, cols] = _silu(_conv3_rows(_mm(y_sc[...], w_ref[:, c0:c1]), tm, cw_ref[:, cols])).astype(BF16)
    dt_raw = _mm(y_hi, wdh_ref[...]) + _mm(y_lo, wdh_ref[...]) + _mm(y_hi, wdl_ref[...])
    dt_ref[...] = _softplus(dt_raw + dtb_ref[...])

    cos = cos_ref[...]
    sin = sin_ref[...]
    seg_ones = seg_ref[...]
    for pair in range(GQA_Q // LANES):
        c0 = SEG_GQ[0] + pair * LANES
        xr = _qk_norm_rope(_mm(y_hi, w_ref[:, c0:c0 + LANES]), qw_ref[...], cos, sin, seg_ones)
        xr = (xr * HEAD_DIM ** -0.5).astype(BF16)
        q_ref[2 * pair] = xr[:, :HEAD_DIM]
        q_ref[2 * pair + 1] = xr[:, HEAD_DIM:]
    kr = _qk_norm_rope(_mm(y_hi, w_ref[:, SEG_GK[0]:SEG_GK[1]]), kw_ref[...], cos, sin, seg_ones).astype(BF16)
    k_ref[0] = kr[:, :HEAD_DIM]
    k_ref[1] = kr[:, HEAD_DIM:]
    vv = _mm(y_hi, w_ref[:, SEG_GV[0]:SEG_GV[1]]).astype(BF16)
    v_ref[0] = vv[:, :HEAD_DIM]
    v_ref[1] = vv[:, HEAD_DIM:]


def _halo_specs(tm, n_tok):
    per_tile = tm // F32_ROWS
    last = n_tok // F32_ROWS - 1
    return [pl.BlockSpec((F32_ROWS, D_MODEL), lambda i: (jnp.maximum(i * per_tile - 1, 0), 0)),
            pl.BlockSpec((F32_ROWS, D_MODEL), lambda i: (jnp.minimum((i + 1) * per_tile, last), 0))]


def _in_proj(h, mod4, row_fn, norm_w, cos, sin, qw, kw, seg_ones, conv_taps, dt_bias, w_main, w_dt_hi, w_dt_lo,
             tm, seq_len):
    n_tok = h.shape[0]
    pos_tiles = seq_len // tm
    tok = lambda width: pl.BlockSpec((tm, width), lambda i: (i, 0))
    hm = lambda nh: pl.BlockSpec((nh, tm, HEAD_DIM), lambda i: (0, i, 0))
    outs = pl.pallas_call(
        functools.partial(_inproj_kernel, tm=tm, tiles_per_seq=pos_tiles),
        out_shape=(
            jax.ShapeDtypeStruct((n_tok, 3 * NA_W), BF16),
            jax.ShapeDtypeStruct((n_tok, SSM_D_INNER), BF16),
            jax.ShapeDtypeStruct((n_tok, SSM_XBC), BF16),
            jax.ShapeDtypeStruct((n_tok, DT_COLS), F32),
            jax.ShapeDtypeStruct((GQA_HEADS, n_tok, HEAD_DIM), BF16),
            jax.ShapeDtypeStruct((GQA_KV_HEADS, n_tok, HEAD_DIM), BF16),
            jax.ShapeDtypeStruct((GQA_KV_HEADS, n_tok, HEAD_DIM), BF16),
            jax.ShapeDtypeStruct((n_tok, 3 * D_MODEL), BF16),
        ),
        grid=(n_tok // tm,),
        in_specs=[
            tok(D_MODEL),
            *_halo_specs(tm, n_tok),
            _mod_spec(0, row_fn),
            _mod_spec(1, row_fn),
            _resident((1, D_MODEL)),
            pl.BlockSpec((tm, LANES), lambda i: (i % pos_tiles, 0)),
            pl.BlockSpec((tm, LANES), lambda i: (i % pos_tiles, 0)),
            _resident((1, LANES)),
            _resident((1, LANES)),
            _resident((2 * LANES, LANES)),
            _resident((F32_ROWS, SSM_XBC)),
            _resident((1, DT_COLS)),
            _resident((D_MODEL, W_MAIN_COLS)),
            _resident((D_MODEL, DT_COLS)),
            _resident((D_MODEL, DT_COLS)),
        ],
        out_specs=(tok(3 * NA_W), tok(SSM_D_INNER), tok(SSM_XBC), tok(DT_COLS),
                   hm(GQA_HEADS), hm(GQA_KV_HEADS), hm(GQA_KV_HEADS), tok(3 * D_MODEL)),
        scratch_shapes=[pltpu.VMEM((tm + 2 * F32_ROWS, D_MODEL), BF16)],
        compiler_params=_params(("arbitrary",)),
        name="in_proj",
    )(h, h, h, mod4, mod4, norm_w, cos, sin, qw, kw, seg_ones, conv_taps, dt_bias, w_main, w_dt_hi, w_dt_lo)
    return outs


def _softmax_pv(s_list, v_list):
    m = s_list[0].max(axis=-1, keepdims=True)
    for s in s_list[1:]:
        m = jnp.maximum(m, s.max(axis=-1, keepdims=True))
    den = 0.0
    out = 0.0
    for s, v in zip(s_list, v_list):
        p = jnp.exp(s - m)
        den = den + p.sum(axis=-1, keepdims=True)
        out = out + _mm(p.astype(BF16), v)
    return out / den


def _na_kernel(q_ref, k_ref, v_ref, kc_ref, vc_ref, bm_ref, o_ref, *, n_rows):
    n_groups = n_rows // NA_ROW_GROUP
    gq = NA_ROW_GROUP * GRID_W
    slab = NA_SLAB_ROWS * GRID_W
    kc = kc_ref[...]
    vc = vc_ref[...]
    head0 = lax.broadcasted_iota(jnp.int32, (1, LANES), 1) < HEAD_DIM

    def group(i, carry):
        r0 = i * NA_ROW_GROUP
        start = jnp.clip(r0 - NA_WIN_ROWS // 2, 0, n_rows - NA_SLAB_ROWS)
        cls = jnp.where(i == 0, 0, jnp.where(i == n_groups - 1, 2, 1))
        q2 = q_ref[pl.ds(pl.multiple_of(r0 * GRID_W, gq), gq), :]
        ks = k_ref[pl.ds(pl.multiple_of(start * GRID_W, GRID_W), slab), :]
        vs = v_ref[pl.ds(pl.multiple_of(start * GRID_W, GRID_W), slab), :]
        zero = jnp.zeros_like(q2)
        qs = jnp.concatenate([jnp.where(head0, q2, zero), jnp.where(head0, zero, q2)], axis=0)
        s_win = _mm_nt(qs, ks) + bm_ref[cls]
        s_ctx = _mm_nt(qs, kc)
        o = _softmax_pv([s_win, s_ctx], [vs, vc])
        o_ref[pl.ds(pl.multiple_of(r0 * GRID_W, gq), gq), :] = jnp.where(head0, o[:gq], o[gq:]).astype(BF16)
        return carry

    lax.fori_loop(0, n_groups, group, 0, unroll=2)


def _neighbourhood_attention(na_l, na_c, bias_mask, n_batch, seq_len, ctx_len):
    n_rows = seq_len // GRID_W
    n_pairs = NA_W // LANES
    lat = lambda part: pl.BlockSpec((seq_len, LANES), lambda b, p: (b, part * n_pairs + p))
    ctx = lambda part: pl.BlockSpec((ctx_len, LANES), lambda b, p: (b, part * n_pairs + p))
    return pl.pallas_call(
        functools.partial(_na_kernel, n_rows=n_rows),
        out_shape=jax.ShapeDtypeStruct((n_batch * seq_len, NA_W), BF16),
        grid=(n_batch, n_pairs),
        in_specs=[lat(0), lat(1), lat(2), ctx(1), ctx(2),
                  pl.BlockSpec((None, 3, 2 * NA_ROW_GROUP * GRID_W, NA_SLAB_ROWS * GRID_W),
                               lambda b, p: (p, 0, 0, 0))],
        out_specs=pl.BlockSpec((seq_len, LANES), lambda b, p: (b, p)),
        compiler_params=_params(("arbitrary", "arbitrary")),
        name="neighbourhood_attention",
    )(na_l, na_l, na_l, na_c, na_c, bias_mask)


def _ctx_mha_kernel(q_ref, k_ref, v_ref, o_ref):
    q2 = q_ref[...]
    k2 = k_ref[...]
    v2 = v_ref[...]
    head0 = lax.broadcasted_iota(jnp.int32, (1, LANES), 1) < HEAD_DIM
    outs = []
    for hh in range(2):
        keep = head0 if hh == 0 else jnp.logical_not(head0)
        qm = jnp.where(keep, q2, jnp.zeros_like(q2))
        outs.append(_softmax_pv([_mm_nt(qm, k2)], [v2]))
    o_ref[...] = jnp.where(head0, outs[0], outs[1]).astype(BF16)


def _context_mha(na_c, n_batch, ctx_len):
    n_pairs = NA_W // LANES
    part = lambda which: pl.BlockSpec((ctx_len, LANES), lambda b, p: (b, which * n_pairs + p))
    return pl.pallas_call(
        _ctx_mha_kernel,
        out_shape=jax.ShapeDtypeStruct((n_batch * ctx_len, NA_W), BF16),
        grid=(n_batch, n_pairs),
        in_specs=[part(0), part(1), part(2)],
        out_specs=pl.BlockSpec((ctx_len, LANES), lambda b, p: (b, p)),
        compiler_params=_params(("arbitrary", "arbitrary")),
        name="context_mha",
    )(na_c, na_c, na_c)


def _gqa_kernel(q_ref, k_ref, v_ref, o_ref, s_sc, p_sc, m_sc, a_sc, acc_sc, *, tq, n_blocks):
    rows = GQA_GROUP * tq
    q = q_ref[...].reshape(rows, HEAD_DIM)
    m_sc[...] = jnp.full_like(m_sc, NEG_BIG)
    acc_sc[...] = jnp.zeros_like(acc_sc)

    def scores(blk, slot):
        s_sc[slot] = _mm_nt(q, k_ref[blk * GQA_BK:(blk + 1) * GQA_BK, :])

    def absorb(blk, slot):
        for c0 in range(0, rows, GQA_ROW_CHUNK):
            rs = slice(c0, c0 + GQA_ROW_CHUNK)
            tiles = [s_sc[slot, rs, t * LANES:(t + 1) * LANES] for t in range(GQA_BK // LANES)]
            m_old = m_sc[rs, :]
            m_new = jnp.maximum(m_old, functools.reduce(jnp.maximum, tiles).max(axis=-1, keepdims=True))
            a_sc[rs, :] = jnp.exp(m_old - m_new)
            m_sc[rs, :] = m_new
            for t, s in enumerate(tiles):
                p_sc[rs, t * LANES:(t + 1) * LANES] = jnp.exp(s - m_new).astype(BF16)
        acc_sc[...] = a_sc[...] * acc_sc[...] + _mm(p_sc[...], v_ref[blk * GQA_BK:(blk + 1) * GQA_BK, :])

    scores(0, 0)
    for j in range(n_blocks):
        if j + 1 < n_blocks:
            scores(j + 1, (j + 1) % 2)
        absorb(j, j % 2)
    acc = acc_sc[...]
    o = acc[:, :HEAD_DIM] / acc[:, HEAD_DIM:HEAD_DIM + 1]
    o_ref[...] = jnp.concatenate([o[g * tq:(g + 1) * tq] for g in range(GQA_GROUP)], axis=-1).astype(BF16)


def _gqa_attention(q_hm, k_all, v_aug, n_batch, q_len, tq):
    q_tiles = q_len // tq
    nk = k_all.shape[2]
    n_blocks = nk // GQA_BK
    assert nk % GQA_BK == 0
    rows = GQA_GROUP * tq
    return pl.pallas_call(
        functools.partial(_gqa_kernel, tq=tq, n_blocks=n_blocks),
        out_shape=jax.ShapeDtypeStruct((n_batch * q_len, GQA_Q), BF16),
        grid=(n_batch, GQA_KV_HEADS, q_tiles),
        in_specs=[
            pl.BlockSpec((GQA_GROUP, tq, HEAD_DIM), lambda b, g, i: (g, b * q_tiles + i, 0)),
            pl.BlockSpec((None, None, nk, HEAD_DIM), lambda b, g, i: (g, b, 0, 0)),
            pl.BlockSpec((None, None, nk, LANES), lambda b, g, i: (g, b, 0, 0)),
        ],
        out_specs=pl.BlockSpec((tq, GQA_GROUP * HEAD_DIM), lambda b, g, i: (b * q_tiles + i, g)),
        scratch_shapes=[
            pltpu.VMEM((2, rows, GQA_BK), F32),
            pltpu.VMEM((rows, GQA_BK), BF16),
            pltpu.VMEM((rows, LANES), F32),
            pltpu.VMEM((rows, LANES), F32),
            pltpu.VMEM((rows, LANES), F32),
        ],
        compiler_params=_params(("arbitrary", "arbitrary", "arbitrary")),
        name="gqa_attention",
    )(q_hm, k_all, v_aug)


def _ssd_kernel(xl_ref, xc_ref, dtl_ref, dtc_ref, zl_ref, zc_ref, alog_ref, dsk_ref, nw_ref, tri_ref, exp_ref,
                outl_ref, outc_ref, x_sc, dt_sc, z_sc, y_sc, st_sc, *, n_chunks, n_ctx_chunks):
    s = pl.program_id(1)
    direction = s // n_chunks
    sp = s % n_chunks
    gid = jnp.where(direction == 0, sp,
                    jnp.where(sp < n_ctx_chunks, n_ctx_chunks - 1 - sp, n_chunks + n_ctx_chunks - 1 - sp))
    is_ctx = gid < n_ctx_chunks

    def stage(x_ref, dt_ref, z_ref):
        x_sc[...] = x_ref[...]
        dt_sc[...] = dt_ref[...]
        z_sc[...] = z_ref[...]

    @pl.when(is_ctx)
    def _():
        stage(xc_ref, dtc_ref, zc_ref)

    @pl.when(jnp.logical_not(is_ctx))
    def _():
        stage(xl_ref, dtl_ref, zl_ref)

    @pl.when(sp == 0)
    def _():
        st_sc[...] = jnp.zeros_like(st_sc)

    tri = tri_ref[direction]
    causal = tri > 0
    neg_a = -jnp.exp(alog_ref[...])
    expand = exp_ref[...]
    heads_per_group = SSM_HEADS // SSM_GROUPS
    group_lanes = heads_per_group * HEAD_DIM

    def chunk_scan(e):
        xs = x_sc[e, :, :SSM_D_INNER]
        dt = dt_sc[e]
        da = dt * neg_a
        da_hi, da_lo = _split_bf16(da)
        cs = _mm(tri, da_hi) + _mm(tri, da_lo)
        tot = da.sum(axis=0, keepdims=True)
        w = dt * jnp.exp(tot - cs)
        e_cs = jnp.exp(cs)
        cs_t = cs.T
        dt_t = dt.T
        w_hi, w_lo = _split_bf16(w)
        w_x = _mm(w_hi, expand) + _mm(w_lo, expand)
        e_hi, e_lo = _split_bf16(e_cs)
        e_cs_x = _mm(e_hi, expand) + _mm(e_lo, expand)
        t_hi, t_lo = _split_bf16(jnp.broadcast_to(jnp.exp(tot), (F32_ROWS, LANES)))
        e_tot_x = (_mm(t_hi, expand) + _mm(t_lo, expand))[0:1]
        xw = (xs.astype(F32) * w_x).astype(BF16)
        y_groups = []
        for g in range(SSM_GROUPS):
            b_g = x_sc[e, :, SSM_D_INNER + g * SSM_D_STATE:SSM_D_INNER + (g + 1) * SSM_D_STATE]
            c_g = x_sc[e, :, SSM_D_INNER + SSM_BC + g * SSM_D_STATE:SSM_D_INNER + SSM_BC + (g + 1) * SSM_D_STATE]
            cb = _mm_nt(c_g, b_g)
            lanes = slice(g * group_lanes, (g + 1) * group_lanes)
            st = st_sc[e, g]
            y_off = _mm(c_g, st.astype(BF16)) * e_cs_x[:, lanes]
            st_sc[e, g] = e_tot_x[:, lanes] * st + _mm(b_g.astype(F32).T.astype(BF16), xw[:, lanes])
            ys = []
            for h in range(g * heads_per_group, (g + 1) * heads_per_group):
                decay = jnp.exp(jnp.where(causal, cs[:, h:h + 1] - cs_t[h:h + 1, :], NEG_BIG))
                ys.append(_mm((cb * decay * dt_t[h:h + 1, :]).astype(BF16), xs[:, h * HEAD_DIM:(h + 1) * HEAD_DIM]))
            y_groups.append(jnp.concatenate(ys, axis=-1) + y_off)
        return jnp.concatenate(y_groups, axis=-1)

    ys = [chunk_scan(e) for e in range(SSD_SEQS)]

    @pl.when(direction == 0)
    def _():
        for e in range(SSD_SEQS):
            y_sc[e, gid] = ys[e] + dsk_ref[...] * x_sc[e, :, :SSM_D_INNER].astype(F32)

    @pl.when(direction == 1)
    def _():
        res = []
        for e in range(SSD_SEQS):
            gated = (y_sc[e, gid] + ys[e]) * _silu(z_sc[e].astype(F32))
            res.append((_rms(gated) * nw_ref[...]).astype(BF16))

        @pl.when(is_ctx)
        def _():
            for e in range(SSD_SEQS):
                outc_ref[e] = res[e]

        @pl.when(jnp.logical_not(is_ctx))
        def _():
            for e in range(SSD_SEQS):
                outl_ref[e] = res[e]


def _ssd_mixer(xbc_l, dt_l, z_l, xbc_c, dt_c, z_c, a_log_p, d_skip_x, norm_w, tri, expand,
               n_batch, seq_len, ctx_len):
    T = SSM_CHUNK
    nlc = seq_len // T
    ncc = ctx_len // T
    n_chunks = nlc + ncc

    def ids(s):
        direction = s // n_chunks
        sp = s % n_chunks
        gid = jnp.where(direction == 0, sp, jnp.where(sp < ncc, ncc - 1 - sp, n_chunks + ncc - 1 - sp))
        return direction, gid, jnp.clip(gid, 0, ncc - 1), jnp.clip(gid - ncc, 0, nlc - 1)

    def lat_main(b, s):
        return (b, ids(s)[3], 0)

    def ctx_main(b, s):
        return (b, ids(s)[2], 0)

    def lat_dt(b, s):
        return (b, ids(s)[3], ids(s)[0])

    def ctx_dt(b, s):
        return (b, ids(s)[2], ids(s)[0])

    def lat_out(b, s):
        direction, gid, _, lc = ids(s)
        return (b, jnp.where(jnp.logical_and(direction == 1, gid >= ncc), lc, nlc - 1), 0)

    def ctx_out(b, s):
        direction, gid, cc, _ = ids(s)
        return (b, jnp.where(direction == 1, jnp.where(gid < ncc, cc, 0), ncc - 1), 0)

    def by_dir(b, s):
        return (ids(s)[0], 0, 0)

    const2 = lambda b, s: (0, 0)
    const3 = lambda b, s: (0, 0, 0)
    seqs = lambda t, n: t.reshape(n_batch, n, t.shape[-1])
    blk = lambda width, index_map: pl.BlockSpec((SSD_SEQS, T, width), index_map)
    out_l, out_c = pl.pallas_call(
        functools.partial(_ssd_kernel, n_chunks=n_chunks, n_ctx_chunks=ncc),
        out_shape=(jax.ShapeDtypeStruct((n_batch, seq_len, SSM_D_INNER), BF16),
                   jax.ShapeDtypeStruct((n_batch, ctx_len, SSM_D_INNER), BF16)),
        grid=(n_batch // SSD_SEQS, 2 * n_chunks),
        in_specs=[
            blk(SSM_XBC, lat_main),
            blk(SSM_XBC, ctx_main),
            blk(LANES, lat_dt),
            blk(LANES, ctx_dt),
            blk(SSM_D_INNER, lat_out),
            blk(SSM_D_INNER, ctx_out),
            pl.BlockSpec((None, 1, LANES), by_dir),
            pl.BlockSpec((1, SSM_D_INNER), const2),
            pl.BlockSpec((1, SSM_D_INNER), const2),
            pl.BlockSpec((2, T, T), const3),
            pl.BlockSpec((LANES, SSM_D_INNER), const2),
        ],
        out_specs=(blk(SSM_D_INNER, lat_out), blk(SSM_D_INNER, ctx_out)),
        scratch_shapes=[
            pltpu.VMEM((SSD_SEQS, T, SSM_XBC), BF16),
            pltpu.VMEM((SSD_SEQS, T, LANES), F32),
            pltpu.VMEM((SSD_SEQS, T, SSM_D_INNER), BF16),
            pltpu.VMEM((SSD_SEQS, n_chunks, T, SSM_D_INNER), F32),
            pltpu.VMEM((SSD_SEQS, SSM_GROUPS, SSM_D_STATE, SSM_D_INNER // SSM_GROUPS), F32),
        ],
        compiler_params=_params(("arbitrary", "arbitrary")),
        name="ssd_mixer",
    )(seqs(xbc_l, seq_len), seqs(xbc_c, ctx_len), seqs(dt_l, seq_len), seqs(dt_c, ctx_len),
      seqs(z_l, seq_len), seqs(z_c, ctx_len), a_log_p, d_skip_x, norm_w, tri, expand)
    return out_l.reshape(n_batch * seq_len, SSM_D_INNER), out_c.reshape(n_batch * ctx_len, SSM_D_INNER)


def _merge_kernel(h_ref, na_ref, ssm_ref, gqa_ref, gate_ref, g_ref, w1_ref, w2_ref, w3_ref, wo_ref, o_ref):
    gates = gate_ref[...]
    t = _sigmoid(gates[:, :D_MODEL].astype(F32)) * _mm(na_ref[...], w1_ref[...])
    t = t + _sigmoid(gates[:, D_MODEL:2 * D_MODEL].astype(F32)) * _mm(ssm_ref[...], w2_ref[...])
    t = t + _sigmoid(gates[:, 2 * D_MODEL:].astype(F32)) * _mm(gqa_ref[...], w3_ref[...])
    o_ref[...] = h_ref[...] + g_ref[...] * _mm(t.astype(BF16), wo_ref[...])


def _merge(h, o_na, o_ssm, o_gqa, gates, mod4, row_fn, w1, w2, w3, wo, tm):
    n_tok = h.shape[0]
    tok = lambda width: pl.BlockSpec((tm, width), lambda i: (i, 0))
    return pl.pallas_call(
        _merge_kernel,
        out_shape=jax.ShapeDtypeStruct((n_tok, D_MODEL), F32),
        grid=(n_tok // tm,),
        in_specs=[tok(D_MODEL), tok(NA_W), tok(SSM_D_INNER), tok(GQA_Q), tok(3 * D_MODEL),
                  _mod_spec(2, row_fn),
                  _resident((NA_W, D_MODEL)), _resident((SSM_D_INNER, D_MODEL)),
                  _resident((GQA_Q, D_MODEL)), _resident((D_MODEL, D_MODEL))],
        out_specs=tok(D_MODEL),
        compiler_params=_params(("arbitrary",)),
        name="branch_merge",
    )(h, o_na, o_ssm, o_gqa, gates, mod4, w1, w2, w3, wo)


def _ffn_kernel(h_ref, hp_ref, hn_ref, sh_ref, sc_ref, g_ref, nw_ref, wup_ref, cw_ref, wdn_ref, fw_ref,
                o_ref, hn_sc, acc_sc, *, tm, tiles_per_seq, final_norm):
    j = pl.program_id(0) % tiles_per_seq
    seq_first = j == 0
    seq_last = j == tiles_per_seq - 1

    def norm_mod(x):
        return (_rms(x) * nw_ref[...]) * (1.0 + sc_ref[...]) + sh_ref[...]

    x = h_ref[...]
    hn_sc[0:tm] = norm_mod(x).astype(BF16)
    before = jnp.where(seq_first, 0.0, norm_mod(hp_ref[...]))
    after = jnp.where(seq_last, 0.0, norm_mod(hn_ref[...]))
    hn_sc[tm:] = jnp.concatenate([before, after], axis=0).astype(BF16)
    acc_sc[...] = jnp.zeros_like(acc_sc)

    def chunk(c, carry):
        y = _conv3_rows(_mm(hn_sc[...], wup_ref[c]), tm, cw_ref[c])
        act = (_silu(y[:, :FFN_TN]) * y[:, FFN_TN:]).astype(BF16)
        acc_sc[...] += _mm(act, wdn_ref[c])
        return carry

    lax.fori_loop(0, FFN_CHUNKS, chunk, 0, unroll=True)
    out = x + g_ref[...] * acc_sc[...]
    if final_norm:
        out = _rms(out) * fw_ref[...]
    o_ref[...] = out


def _conv_ffn(h, mod4, row_fn, norm_w, w_up_c, conv_c, w_down_c, final_w, tm, seq_len, final_norm):
    n_tok = h.shape[0]
    return pl.pallas_call(
        functools.partial(_ffn_kernel, tm=tm, tiles_per_seq=seq_len // tm, final_norm=final_norm),
        out_shape=jax.ShapeDtypeStruct((n_tok, D_MODEL), F32),
        grid=(n_tok // tm,),
        in_specs=[
            pl.BlockSpec((tm, D_MODEL), lambda i: (i, 0)),
            *_halo_specs(tm, n_tok),
            _mod_spec(3, row_fn),
            _mod_spec(4, row_fn),
            _mod_spec(5, row_fn),
            _resident((1, D_MODEL)),
            _resident((FFN_CHUNKS, D_MODEL, 2 * FFN_TN)),
            _resident((FFN_CHUNKS, F32_ROWS, 2 * FFN_TN)),
            _resident((FFN_CHUNKS, FFN_TN, D_MODEL)),
            _resident((1, D_MODEL)),
        ],
        out_specs=pl.BlockSpec((tm, D_MODEL), lambda i: (i, 0)),
        scratch_shapes=[pltpu.VMEM((tm + 2 * F32_ROWS, D_MODEL), BF16), pltpu.VMEM((tm, D_MODEL), F32)],
        compiler_params=_params(("arbitrary",)),
        name="conv_ffn",
    )(h, h, h, mod4, mod4, mod4, norm_w, w_up_c, conv_c, w_down_c, final_w)


def _rope_tables(seq_len):
    n_freq = HEAD_DIM // 4
    inv_freq = ROPE_THETA ** (-jnp.arange(n_freq, dtype=F32) / n_freq)
    t = jnp.arange(seq_len, dtype=jnp.int32)
    row = (t // GRID_W).astype(F32)
    col = (t % GRID_W).astype(F32)
    ang = jnp.concatenate([row[:, None] * inv_freq, col[:, None] * inv_freq], axis=-1)
    cos = jnp.repeat(jnp.cos(ang), 2, axis=-1)
    sin = jnp.repeat(jnp.sin(ang), 2, axis=-1) * jnp.tile(jnp.asarray([-1.0, 1.0], F32), HEAD_DIM // 2)
    return jnp.tile(cos, (1, LANES // HEAD_DIM)), jnp.tile(sin, (1, LANES // HEAD_DIM))


def _na_bias_mask(rel_bias):
    qc = np.arange(GRID_W)[:, None]
    kc = np.arange(GRID_W)[None, :]
    win0 = np.clip(qc - NA_WIN_COLS // 2, 0, GRID_W - NA_WIN_COLS)
    ok = (kc >= win0) & (kc < win0 + NA_WIN_COLS)
    dcol = np.clip(kc - qc + NA_WIN_COLS - 1, 0, 2 * NA_WIN_COLS - 2)
    per_drow = jnp.where(jnp.asarray(ok)[None, None], rel_bias.astype(F32)[:, :, dcol], NEG_BIG)
    n_heads = rel_bias.shape[0]
    masked = jnp.full((n_heads, GRID_W, GRID_W), NEG_BIG, F32)
    half = NA_WIN_ROWS // 2
    classes = []
    for cls in range(3):
        rows = []
        for j in range(NA_ROW_GROUP):
            first_slab_row = (0, j, NA_SLAB_ROWS - NA_WIN_ROWS)[cls]
            drow0 = (NA_WIN_ROWS - 1 - j, NA_WIN_ROWS - 1 - half, NA_WIN_ROWS - 1 - half - j)[cls]
            blocks = [per_drow[:, drow0 + sr - first_slab_row] if 0 <= sr - first_slab_row < NA_WIN_ROWS else masked
                      for sr in range(NA_SLAB_ROWS)]
            rows.append(jnp.concatenate(blocks, axis=-1))
        classes.append(jnp.concatenate(rows, axis=1))
    table = jnp.stack(classes, axis=1)
    table = table.reshape(n_heads // 2, 2, 3, NA_ROW_GROUP * GRID_W, NA_SLAB_ROWS * GRID_W)
    return jnp.moveaxis(table, 1, 2).reshape(n_heads // 2, 3, 2 * NA_ROW_GROUP * GRID_W, NA_SLAB_ROWS * GRID_W)


def _pack_w_in(w_in):
    sizes = [3 * NA_W, SSM_D_INNER, SSM_XBC, 2 * SSM_HEADS, GQA_Q, GQA_KV, GQA_KV]
    na, z, xbc, dt, gq, gk, gv, gate = jnp.split(w_in, [int(i) for i in np.cumsum(sizes)], axis=-1)
    na = jnp.concatenate([na[:, :NA_W] * HEAD_DIM ** -0.5, na[:, NA_W:]], axis=-1)
    w_main = jnp.concatenate([na, z, xbc, gq, gk, gv, gate], axis=-1).astype(BF16)
    dt_pad = jnp.zeros((D_MODEL, DT_COLS), F32)
    dt_pad = dt_pad.at[:, :SSM_HEADS].set(dt[:, :SSM_HEADS]).at[:, LANES:LANES + SSM_HEADS].set(dt[:, SSM_HEADS:])
    dt_hi, dt_lo = _split_bf16(dt_pad)
    return w_main, dt_hi, dt_lo


def _pad_heads(v):
    return jnp.zeros((2, 1, LANES), F32).at[:, 0, :SSM_HEADS].set(v.astype(F32))


def _pack_ffn(w_up, conv_w, conv_b, w_down):
    def chunked(t):
        a, b = t[..., :FFN_HIDDEN], t[..., FFN_HIDDEN:]
        lead = t.shape[:-1]
        return jnp.concatenate([a.reshape(lead + (FFN_CHUNKS, FFN_TN)), b.reshape(lead + (FFN_CHUNKS, FFN_TN))], axis=-1)

    w_up_c = jnp.moveaxis(chunked(w_up), 1, 0).astype(BF16)
    taps = jnp.concatenate([conv_w, conv_b[None], jnp.zeros((F32_ROWS - 4, 2 * FFN_HIDDEN), F32)], axis=0)
    conv_c = jnp.moveaxis(chunked(taps), 1, 0)
    w_down_c = w_down.reshape(FFN_CHUNKS, FFN_TN, D_MODEL).astype(BF16)
    return w_up_c, conv_c, w_down_c


def _scan_masks():
    t = np.arange(SSM_CHUNK)
    tril = (t[:, None] >= t[None, :]).astype(np.float32)
    return jnp.asarray(np.stack([tril, tril.T]), BF16)


def kernel(x, c, ctx, c_ctx, w_mod, b_mod, norm1_w, norm2_w, w_in, na_rel_bias, ssm_conv_w, ssm_conv_b, ssm_a_log, ssm_dt_bias, ssm_d, ssm_norm_w, q_norm_w, k_norm_w, w_out_na, w_out_ssm, w_out_gqa, w_o, ffn_w_up, ffn_conv_w, ffn_conv_b, ffn_w_down, final_norm_w):
    n_batch, seq_len, _ = x.shape
    ctx_len = ctx.shape[1]
    depth = w_mod.shape[0]
    assert seq_len % (GRID_W * NA_ROW_GROUP) == 0 and seq_len // GRID_W >= NA_SLAB_ROWS
    assert n_batch <= CTX_MOD_ROW and n_batch % SSD_SEQS == 0 and seq_len % 512 == 0 and ctx_len % 256 == 0
    tm_lat, tm_ctx = 512, 256

    cc = jnp.zeros((MOD_ROWS, D_MODEL), F32).at[:n_batch].set(c).at[CTX_MOD_ROW].set(c_ctx)
    mods = _mod_vectors(cc, w_mod, b_mod)
    lat_row = lambda i: i // (seq_len // tm_lat)
    ctx_row = lambda i: CTX_MOD_ROW

    cos_l, sin_l = _rope_tables(seq_len)
    cos_c, sin_c = jnp.ones((ctx_len, LANES), F32), jnp.zeros((ctx_len, LANES), F32)
    head_of_lane = np.arange(LANES) // HEAD_DIM
    same_head = head_of_lane[:, None] == head_of_lane[None, :]
    seg_ones = jnp.asarray(np.concatenate([same_head, same_head], axis=0), BF16)
    tri = _scan_masks()
    lane_head = np.arange(SSM_D_INNER) // HEAD_DIM
    expand = jnp.asarray(np.arange(LANES)[:, None] == lane_head[None, :], BF16)

    h_lat = x.reshape(n_batch * seq_len, D_MODEL)
    h_ctx = ctx.reshape(n_batch * ctx_len, D_MODEL)
    row = lambda v: v.reshape(1, -1).astype(F32)
    for layer in range(depth):
        need_ctx = layer < depth - 1
        last = layer == depth - 1
        mod4 = mods[layer].reshape(MOD_ROWS, N_MOD, 1, D_MODEL)
        w_main, w_dt_hi, w_dt_lo = _pack_w_in(w_in[layer])
        qw = row(jnp.tile(q_norm_w[layer], LANES // HEAD_DIM))
        kw = row(jnp.tile(k_norm_w[layer], LANES // HEAD_DIM))
        conv_taps = jnp.concatenate(
            [ssm_conv_w[layer], ssm_conv_b[layer][None], jnp.zeros((F32_ROWS - 4, SSM_XBC), F32)], axis=0)
        dt_bias = jnp.zeros((1, DT_COLS), F32)
        dt_bias = dt_bias.at[0, :SSM_HEADS].set(ssm_dt_bias[layer, 0]).at[0, LANES:LANES + SSM_HEADS].set(
            ssm_dt_bias[layer, 1])
        proj = functools.partial(_in_proj, mod4=mod4, norm_w=row(norm1_w[layer]), qw=qw, kw=kw, seg_ones=seg_ones,
                                 conv_taps=conv_taps, dt_bias=dt_bias,
                                 w_main=w_main, w_dt_hi=w_dt_hi, w_dt_lo=w_dt_lo)
        na_l, z_l, xbc_l, dt_l, q_l, k_l, v_l, gate_l = proj(
            h_lat, row_fn=lat_row, cos=cos_l, sin=sin_l, tm=tm_lat, seq_len=seq_len)
        na_c, z_c, xbc_c, dt_c, q_c, k_c, v_c, gate_c = proj(
            h_ctx, row_fn=ctx_row, cos=cos_c, sin=sin_c, tm=tm_ctx, seq_len=ctx_len)

        o_na_l = _neighbourhood_attention(na_l, na_c, _na_bias_mask(na_rel_bias[layer]), n_batch, seq_len, ctx_len)
        o_ssm_l, o_ssm_c = _ssd_mixer(
            xbc_l, dt_l, z_l, xbc_c, dt_c, z_c, _pad_heads(ssm_a_log[layer]),
            row(jnp.repeat(ssm_d[layer], HEAD_DIM)), row(ssm_norm_w[layer]), tri, expand,
            n_batch, seq_len, ctx_len)
        per_batch = lambda t, n: t.reshape(GQA_KV_HEADS, n_batch, n, HEAD_DIM)
        with_ones = lambda v: jnp.concatenate(
            [v, jnp.ones(v.shape[:-1] + (1,), BF16), jnp.zeros(v.shape[:-1] + (LANES - HEAD_DIM - 1,), BF16)], axis=-1)
        k_ctx, v_ctx = per_batch(k_c, ctx_len), with_ones(per_batch(v_c, ctx_len))
        k_all = jnp.concatenate([k_ctx, per_batch(k_l, seq_len)], axis=2)
        v_all = jnp.concatenate([v_ctx, with_ones(per_batch(v_l, seq_len))], axis=2)
        o_gqa_l = _gqa_attention(q_l, k_all, v_all, n_batch, seq_len, tq=256)

        w1, w2, w3, wo = (w.astype(BF16) for w in (w_out_na[layer], w_out_ssm[layer], w_out_gqa[layer], w_o[layer]))
        ffn_w = _pack_ffn(ffn_w_up[layer], ffn_conv_w[layer], ffn_conv_b[layer], ffn_w_down[layer])
        h_lat = _merge(h_lat, o_na_l, o_ssm_l, o_gqa_l, gate_l, mod4, lat_row, w1, w2, w3, wo, tm_lat)
        h_lat = _conv_ffn(h_lat, mod4, lat_row, row(norm2_w[layer]), *ffn_w, row(final_norm_w),
                          tm=tm_lat, seq_len=seq_len, final_norm=last)
        if need_ctx:
            o_na_c = _context_mha(na_c, n_batch, ctx_len)
            o_gqa_c = _gqa_attention(q_c, k_ctx, v_ctx, n_batch, ctx_len, tq=ctx_len)
            h_ctx = _merge(h_ctx, o_na_c, o_ssm_c, o_gqa_c, gate_c, mod4, ctx_row, w1, w2, w3, wo, tm_ctx)
            h_ctx = _conv_ffn(h_ctx, mod4, ctx_row, row(norm2_w[layer]), *ffn_w, row(final_norm_w),
                              tm=tm_ctx, seq_len=ctx_len, final_norm=False)
    return h_lat.reshape(n_batch, seq_len, D_MODEL)
```

```python
import functools

import numpy as np
import jax
import jax.numpy as jnp
from jax import lax
from jax.experimental import pallas as pl
from jax.experimental.pallas import tpu as pltpu

F32 = jnp.float32
BF16 = jnp.bfloat16

D_MODEL = 1024
GRID_W = 64
EPS = 1e-6
HEAD_DIM = 64
NA_HEADS = 8
NA_W = NA_HEADS * HEAD_DIM
NA_WIN_ROWS = 8
NA_WIN_COLS = 16
NA_ROW_GROUP = 4
NA_SLAB_ROWS = 12
SSM_HEADS = 16
SSM_D_INNER = SSM_HEADS * HEAD_DIM
SSM_GROUPS = 2
SSM_D_STATE = 64
SSM_BC = SSM_GROUPS * SSM_D_STATE
SSM_XBC = SSM_D_INNER + 2 * SSM_BC
SSM_CHUNK = 128
SSD_SEQS = 2
GQA_HEADS = 8
GQA_KV_HEADS = 2
GQA_GROUP = GQA_HEADS // GQA_KV_HEADS
GQA_Q = GQA_HEADS * HEAD_DIM
GQA_KV = GQA_KV_HEADS * HEAD_DIM
GQA_BK = 512
GQA_ROW_CHUNK = 64
ROPE_THETA = 10000.0
FFN_HIDDEN = 2816
N_MOD = 6
MOD_ROWS = 16
CTX_MOD_ROW = 8
NEG_BIG = -1e30

LANES = 128
BF16_ROWS = 16
F32_ROWS = 8
VMEM_LIMIT = 56 * 1024 * 1024

SEG_NA = (0, 3 * NA_W)
SEG_Z = (SEG_NA[1], SEG_NA[1] + SSM_D_INNER)
SEG_XBC = (SEG_Z[1], SEG_Z[1] + SSM_XBC)
SEG_GQ = (SEG_XBC[1], SEG_XBC[1] + GQA_Q)
SEG_GK = (SEG_GQ[1], SEG_GQ[1] + GQA_KV)
SEG_GV = (SEG_GK[1], SEG_GK[1] + GQA_KV)
SEG_GATE = (SEG_GV[1], SEG_GV[1] + 3 * D_MODEL)
W_MAIN_COLS = SEG_GATE[1]
DT_COLS = 2 * LANES
DOT_COLS = 512

FFN_TN = 256
FFN_CHUNKS = FFN_HIDDEN // FFN_TN


def _mm(a, b):
    return jnp.dot(a, b, preferred_element_type=F32)


def _mm_nt(a, b):
    return lax.dot_general(a, b, (((1,), (1,)), ((), ())), preferred_element_type=F32)


def _split_bf16(x):
    hi = x.astype(BF16)
    lo = (x - hi.astype(F32)).astype(BF16)
    return hi, lo


def _sigmoid(x):
    return 1.0 / (1.0 + jnp.exp(-x))


def _silu(x):
    return x * _sigmoid(x)


def _softplus(x):
    return jnp.maximum(x, 0.0) + jnp.log1p(jnp.exp(-jnp.abs(x)))


def _rms(x):
    return x * lax.rsqrt(jnp.mean(x * x, axis=-1, keepdims=True) + EPS)


def _resident(shape):
    nd = len(shape)
    return pl.BlockSpec(shape, lambda *_: (0,) * nd, pipeline_mode=pl.Buffered(1))


def _params(sem):
    return pltpu.CompilerParams(dimension_semantics=sem, vmem_limit_bytes=VMEM_LIMIT)


def _mod_kernel(c_ref, w_ref, b_ref, o_ref):
    x_hi, x_lo = _split_bf16(_silu(c_ref[...]))
    w_hi, w_lo = _split_bf16(w_ref[...])
    o_ref[...] = _mm(x_hi, w_hi) + _mm(x_lo, w_hi) + _mm(x_hi, w_lo) + b_ref[...]


def _mod_vectors(cc, w_mod, b_mod):
    n_layers = w_mod.shape[0]
    tn = D_MODEL
    return pl.pallas_call(
        _mod_kernel,
        out_shape=jax.ShapeDtypeStruct((n_layers, MOD_ROWS, N_MOD * D_MODEL), F32),
        grid=(n_layers, N_MOD),
        in_specs=[
            pl.BlockSpec((MOD_ROWS, D_MODEL), lambda l, j: (0, 0)),
            pl.BlockSpec((None, D_MODEL, tn), lambda l, j: (l, 0, j)),
            pl.BlockSpec((None, 1, tn), lambda l, j: (l, 0, j)),
        ],
        out_specs=pl.BlockSpec((None, MOD_ROWS, tn), lambda l, j: (l, 0, j)),
        compiler_params=_params(("arbitrary", "arbitrary")),
        name="mod_vectors",
    )(cc, w_mod, b_mod.reshape(n_layers, 1, N_MOD * D_MODEL))


def _mod_spec(which, row_fn):
    return pl.BlockSpec((None, None, 1, D_MODEL), lambda i: (row_fn(i), which, 0, 0))


def _qk_norm_rope(x, w, cos, sin, seg_ones):
    ss = _mm(jnp.concatenate(_split_bf16(x * x), axis=1), seg_ones)
    xn = x * lax.rsqrt(ss * (1.0 / HEAD_DIM) + EPS) * w
    lane = lax.broadcasted_iota(jnp.int32, xn.shape, 1)
    partner = jnp.where((lane & 1) == 0, pltpu.roll(xn, LANES - 1, 1), pltpu.roll(xn, 1, 1))
    return xn * cos + partner * sin


def _conv3_rows(u, tm, taps):
    n = u.shape[0]
    return (taps[0:1] * pltpu.roll(u, 1, 0)[0:tm] + taps[1:2] * u[0:tm]
            + taps[2:3] * pltpu.roll(u, n - 1, 0)[0:tm] + taps[3:4])


def _inproj_kernel(h_ref, hp_ref, hn_ref, sh_ref, sc_ref, nw_ref, cos_ref, sin_ref, qw_ref, kw_ref, seg_ref,
                   cw_ref, dtb_ref, w_ref, wdh_ref, wdl_ref,
                   na_ref, z_ref, xbc_ref, dt_ref, q_ref, k_ref, v_ref, gate_ref, y_sc, *, tm, tiles_per_seq):
    j = pl.program_id(0) % tiles_per_seq

    def norm_mod(x):
        return (_rms(x) * nw_ref[...]) * (1.0 + sc_ref[...]) + sh_ref[...]

    y = norm_mod(h_ref[...])
    y_hi, y_lo = _split_bf16(y)
    before = jnp.where(j == 0, 0.0, norm_mod(hp_ref[...]))
    after = jnp.where(j == tiles_per_seq - 1, 0.0, norm_mod(hn_ref[...]))
    y_sc[0:tm] = y_hi
    y_sc[tm:] = jnp.concatenate([after, before], axis=0).astype(BF16)

    def project(out_ref, seg):
        for c0 in range(seg[0], seg[1], DOT_COLS):
            c1 = min(c0 + DOT_COLS, seg[1])
            out_ref[:, c0 - seg[0]:c1 - seg[0]] = _mm(y_hi, w_ref[:, c0:c1]).astype(out_ref.dtype)

    for c0 in range(SEG_XBC[0], SEG_XBC[1], DOT_COLS):
        c1 = min(c0 + DOT_COLS, SEG_XBC[1])
        cols = slice(c0 - SEG_XBC[0], c1 - SEG_XBC[0])
        xbc_ref[:, cols] = _silu(_conv3_rows(_mm(y_sc[...], w_ref[:, c0:c1]), tm, cw_ref[:, cols])).astype(BF16)
    dt_raw = _mm(y_hi, wdh_ref[...]) + _mm(y_lo, wdh_ref[...]) + _mm(y_hi, wdl_ref[...])
    dt_ref[...] = _softplus(dt_raw + dtb_ref[...])

    cos = cos_ref[...]
    sin = sin_ref[...]
    seg_ones = seg_ref[...]
    for pair in range(GQA_Q // LANES):
        c0 = SEG_GQ[0] + pair * LANES
        xr = _qk_norm_rope(_mm(y_hi, w_ref[:, c0:c0 + LANES]), qw_ref[...], cos, sin, seg_ones)
        xr = (xr * HEAD_DIM ** -0.5).astype(BF16)
        q_ref[2 * pair] = xr[:, :HEAD_DIM]
        q_ref[2 * pair + 1] = xr[:, HEAD_DIM:]
    kr = _qk_norm_rope(_mm(y_hi, w_ref[:, SEG_GK[0]:SEG_GK[1]]), kw_ref[...], cos, sin, seg_ones).astype(BF16)
    k_ref[0] = kr[:, :HEAD_DIM]
    k_ref[1] = kr[:, HEAD_DIM:]
    vv = _mm(y_hi, w_ref[:, SEG_GV[0]:SEG_GV[1]]).astype(BF16)
    v_ref[0] = vv[:, :HEAD_DIM]
    v_ref[1] = vv[:, HEAD_DIM:]
    project(na_ref, SEG_NA)
    project(z_ref, SEG_Z)
    project(gate_ref, SEG_GATE)


def _halo_specs(tm, n_tok):
    per_tile = tm // F32_ROWS
    last = n_tok // F32_ROWS - 1
    return [pl.BlockSpec((F32_ROWS, D_MODEL), lambda i: (jnp.maximum(i * per_tile - 1, 0), 0)),
            pl.BlockSpec((F32_ROWS, D_MODEL), lambda i: (jnp.minimum((i + 1) * per_tile, last), 0))]


def _in_proj(h, mod4, row_fn, norm_w, cos, sin, qw, kw, seg_ones, conv_taps, dt_bias, w_main, w_dt_hi, w_dt_lo,
             tm, seq_len):
    n_tok = h.shape[0]
    pos_tiles = seq_len // tm
    tok = lambda width: pl.BlockSpec((tm, width), lambda i: (i, 0))
    hm = lambda nh: pl.BlockSpec((nh, tm, HEAD_DIM), lambda i: (0, i, 0))
    outs = pl.pallas_call(
        functools.partial(_inproj_kernel, tm=tm, tiles_per_seq=pos_tiles),
        out_shape=(
            jax.ShapeDtypeStruct((n_tok, 3 * NA_W), BF16),
            jax.ShapeDtypeStruct((n_tok, SSM_D_INNER), BF16),
            jax.ShapeDtypeStruct((n_tok, SSM_XBC), BF16),
            jax.ShapeDtypeStruct((n_tok, DT_COLS), F32),
            jax.ShapeDtypeStruct((GQA_HEADS, n_tok, HEAD_DIM), BF16),
            jax.ShapeDtypeStruct((GQA_KV_HEADS, n_tok, HEAD_DIM), BF16),
            jax.ShapeDtypeStruct((GQA_KV_HEADS, n_tok, HEAD_DIM), BF16),
            jax.ShapeDtypeStruct((n_tok, 3 * D_MODEL), BF16),
        ),
        grid=(n_tok // tm,),
        in_specs=[
            tok(D_MODEL),
            *_halo_specs(tm, n_tok),
            _mod_spec(0, row_fn),
            _mod_spec(1, row_fn),
            _resident((1, D_MODEL)),
            pl.BlockSpec((tm, LANES), lambda i: (i % pos_tiles, 0)),
            pl.BlockSpec((tm, LANES), lambda i: (i % pos_tiles, 0)),
            _resident((1, LANES)),
            _resident((1, LANES)),
            _resident((2 * LANES, LANES)),
            _resident((F32_ROWS, SSM_XBC)),
            _resident((1, DT_COLS)),
            _resident((D_MODEL, W_MAIN_COLS)),
            _resident((D_MODEL, DT_COLS)),
            _resident((D_MODEL, DT_COLS)),
        ],
        out_specs=(tok(3 * NA_W), tok(SSM_D_INNER), tok(SSM_XBC), tok(DT_COLS),
                   hm(GQA_HEADS), hm(GQA_KV_HEADS), hm(GQA_KV_HEADS), tok(3 * D_MODEL)),
        scratch_shapes=[pltpu.VMEM((tm + 2 * F32_ROWS, D_MODEL), BF16)],
        compiler_params=_params(("arbitrary",)),
        name="in_proj",
    )(h, h, h, mod4, mod4, norm_w, cos, sin, qw, kw, seg_ones, conv_taps, dt_bias, w_main, w_dt_hi, w_dt_lo)
    return outs


def _softmax_pv(s_list, v_list):
    m = s_list[0].max(axis=-1, keepdims=True)
    for s in s_list[1:]:
        m = jnp.maximum(m, s.max(axis=-1, keepdims=True))
    den = 0.0
    out = 0.0
    for s, v in zip(s_list, v_list):
        p = jnp.exp(s - m)
        den = den + p.sum(axis=-1, keepdims=True)
        out = out + _mm(p.astype(BF16), v)
    return out / den


def _na_kernel(q_ref, k_ref, v_ref, kc_ref, vc_ref, bm_ref, o_ref, *, n_rows):
    n_groups = n_rows // NA_ROW_GROUP
    gq = NA_ROW_GROUP * GRID_W
    slab = NA_SLAB_ROWS * GRID_W
    kc = kc_ref[...]
    vc = vc_ref[...]
    head0 = lax.broadcasted_iota(jnp.int32, (1, LANES), 1) < HEAD_DIM

    def group(i, carry):
        r0 = i * NA_ROW_GROUP
        start = jnp.clip(r0 - NA_WIN_ROWS // 2, 0, n_rows - NA_SLAB_ROWS)
        cls = jnp.where(i == 0, 0, jnp.where(i == n_groups - 1, 2, 1))
        q2 = q_ref[pl.ds(pl.multiple_of(r0 * GRID_W, gq), gq), :]
        ks = k_ref[pl.ds(pl.multiple_of(start * GRID_W, GRID_W), slab), :]
        vs = v_ref[pl.ds(pl.multiple_of(start * GRID_W, GRID_W), slab), :]
        zero = jnp.zeros_like(q2)
        qs = jnp.concatenate([jnp.where(head0, q2, zero), jnp.where(head0, zero, q2)], axis=0)
        s_win = _mm_nt(qs, ks) + bm_ref[cls]
        s_ctx = _mm_nt(qs, kc)
        o = _softmax_pv([s_win, s_ctx], [vs, vc])
        o_ref[pl.ds(pl.multiple_of(r0 * GRID_W, gq), gq), :] = jnp.where(head0, o[:gq], o[gq:]).astype(BF16)
        return carry

    lax.fori_loop(0, n_groups, group, 0, unroll=8)


def _neighbourhood_attention(na_l, na_c, bias_mask, n_batch, seq_len, ctx_len):
    n_rows = seq_len // GRID_W
    n_pairs = NA_W // LANES
    lat = lambda part: pl.BlockSpec((seq_len, LANES), lambda b, p: (b, part * n_pairs + p))
    ctx = lambda part: pl.BlockSpec((ctx_len, LANES), lambda b, p: (b, part * n_pairs + p))
    return pl.pallas_call(
        functools.partial(_na_kernel, n_rows=n_rows),
        out_shape=jax.ShapeDtypeStruct((n_batch * seq_len, NA_W), BF16),
        grid=(n_batch, n_pairs),
        in_specs=[lat(0), lat(1), lat(2), ctx(1), ctx(2),
                  pl.BlockSpec((None, 3, 2 * NA_ROW_GROUP * GRID_W, NA_SLAB_ROWS * GRID_W),
                               lambda b, p: (p, 0, 0, 0))],
        out_specs=pl.BlockSpec((seq_len, LANES), lambda b, p: (b, p)),
        compiler_params=_params(("arbitrary", "arbitrary")),
        name="neighbourhood_attention",
    )(na_l, na_l, na_l, na_c, na_c, bias_mask)


def _ctx_mha_kernel(q_ref, k_ref, v_ref, o_ref):
    q2 = q_ref[...]
    k2 = k_ref[...]
    v2 = v_ref[...]
    head0 = lax.broadcasted_iota(jnp.int32, (1, LANES), 1) < HEAD_DIM
    outs = []
    for hh in range(2):
        keep = head0 if hh == 0 else jnp.logical_not(head0)
        qm = jnp.where(keep, q2, jnp.zeros_like(q2))
        outs.append(_softmax_pv([_mm_nt(qm, k2)], [v2]))
    o_ref[...] = jnp.where(head0, outs[0], outs[1]).astype(BF16)


def _context_mha(na_c, n_batch, ctx_len):
    n_pairs = NA_W // LANES
    part = lambda which: pl.BlockSpec((ctx_len, LANES), lambda b, p: (b, which * n_pairs + p))
    return pl.pallas_call(
        _ctx_mha_kernel,
        out_shape=jax.ShapeDtypeStruct((n_batch * ctx_len, NA_W), BF16),
        grid=(n_batch, n_pairs),
        in_specs=[part(0), part(1), part(2)],
        out_specs=pl.BlockSpec((ctx_len, LANES), lambda b, p: (b, p)),
        compiler_params=_params(("arbitrary", "arbitrary")),
        name="context_mha",
    )(na_c, na_c, na_c)


def _gqa_kernel(q_ref, k_ref, v_ref, o_ref, s_sc, p_sc, m_sc, a_sc, acc_sc, *, tq, blocks):
    rows = GQA_GROUP * tq
    q = q_ref[...].reshape(rows, HEAD_DIM)
    m_sc[...] = jnp.full_like(m_sc, NEG_BIG)
    acc_sc[...] = jnp.zeros_like(acc_sc)

    def scores(blk, slot):
        k0, size = blk
        s_sc[slot, :, 0:size] = _mm_nt(q, k_ref[k0:k0 + size, :])

    def absorb(blk, slot):
        k0, size = blk
        for c0 in range(0, rows, GQA_ROW_CHUNK):
            rs = slice(c0, c0 + GQA_ROW_CHUNK)
            tiles = [s_sc[slot, rs, t * LANES:(t + 1) * LANES] for t in range(size // LANES)]
            m_old = m_sc[rs, :]
            m_new = jnp.maximum(m_old, functools.reduce(jnp.maximum, tiles).max(axis=-1, keepdims=True))
            a_sc[rs, :] = jnp.exp(m_old - m_new)
            m_sc[rs, :] = m_new
            for t, s in enumerate(tiles):
                p_sc[rs, t * LANES:(t + 1) * LANES] = jnp.exp(s - m_new).astype(BF16)
        acc_sc[...] = a_sc[...] * acc_sc[...] + _mm(p_sc[:, 0:size], v_ref[k0:k0 + size, :])

    scores(blocks[0], 0)
    for j, blk in enumerate(blocks):
        if j + 1 < len(blocks):
            scores(blocks[j + 1], (j + 1) % 2)
        absorb(blk, j % 2)
    acc = acc_sc[...]
    o = acc[:, :HEAD_DIM] / acc[:, HEAD_DIM:HEAD_DIM + 1]
    o_ref[...] = jnp.concatenate([o[g * tq:(g + 1) * tq] for g in range(GQA_GROUP)], axis=-1).astype(BF16)


def _gqa_attention(q_hm, k_all, v_aug, n_batch, q_len, tq):
    q_tiles = q_len // tq
    nk = k_all.shape[2]
    assert nk % (2 * LANES) == 0
    first = nk % GQA_BK
    blocks = ([(0, first)] if first else []) + [(k0, GQA_BK) for k0 in range(first, nk, GQA_BK)]
    rows = GQA_GROUP * tq
    return pl.pallas_call(
        functools.partial(_gqa_kernel, tq=tq, blocks=tuple(blocks)),
        out_shape=jax.ShapeDtypeStruct((n_batch * q_len, GQA_Q), BF16),
        grid=(n_batch, GQA_KV_HEADS, q_tiles),
        in_specs=[
            pl.BlockSpec((GQA_GROUP, tq, HEAD_DIM), lambda b, g, i: (g, b * q_tiles + i, 0)),
            pl.BlockSpec((None, None, nk, HEAD_DIM), lambda b, g, i: (g, b, 0, 0)),
            pl.BlockSpec((None, None, nk, LANES), lambda b, g, i: (g, b, 0, 0)),
        ],
        out_specs=pl.BlockSpec((tq, GQA_GROUP * HEAD_DIM), lambda b, g, i: (b * q_tiles + i, g)),
        scratch_shapes=[
            pltpu.VMEM((2, rows, GQA_BK), F32),
            pltpu.VMEM((rows, GQA_BK), BF16),
            pltpu.VMEM((rows, LANES), F32),
            pltpu.VMEM((rows, LANES), F32),
            pltpu.VMEM((rows, LANES), F32),
        ],
        compiler_params=_params(("arbitrary", "arbitrary", "arbitrary")),
        name="gqa_attention",
    )(q_hm, k_all, v_aug)


def _ssd_kernel(xl_ref, xc_ref, dtl_ref, dtc_ref, zl_ref, zc_ref, alog_ref, dsk_ref, nw_ref, tri_ref, exp_ref,
                outl_ref, outc_ref, x_sc, dt_sc, z_sc, y_sc, st_sc, *, n_chunks, n_ctx_chunks):
    s = pl.program_id(1)
    direction = s // n_chunks
    sp = s % n_chunks
    gid = jnp.where(direction == 0, sp,
                    jnp.where(sp < n_ctx_chunks, n_ctx_chunks - 1 - sp, n_chunks + n_ctx_chunks - 1 - sp))
    is_ctx = gid < n_ctx_chunks

    def stage(x_ref, dt_ref, z_ref):
        x_sc[...] = x_ref[...]
        dt_sc[...] = dt_ref[...]
        z_sc[...] = z_ref[...]

    @pl.when(is_ctx)
    def _():
        stage(xc_ref, dtc_ref, zc_ref)

    @pl.when(jnp.logical_not(is_ctx))
    def _():
        stage(xl_ref, dtl_ref, zl_ref)

    @pl.when(sp == 0)
    def _():
        st_sc[...] = jnp.zeros_like(st_sc)

    tri = tri_ref[direction]
    causal = tri > 0
    neg_a = -jnp.exp(alog_ref[...])
    expand = exp_ref[...]
    heads_per_group = SSM_HEADS // SSM_GROUPS
    group_lanes = heads_per_group * HEAD_DIM

    def chunk_scan(e):
        xs = x_sc[e, :, :SSM_D_INNER]
        dt = dt_sc[e]
        da = dt * neg_a
        da_hi, da_lo = _split_bf16(da)
        cs = _mm(tri, da_hi) + _mm(tri, da_lo)
        tot = da.sum(axis=0, keepdims=True)
        w = dt * jnp.exp(tot - cs)
        e_cs = jnp.exp(cs)
        cs_t = cs.T
        dt_t = dt.T
        w_hi, w_lo = _split_bf16(w)
        w_x = _mm(w_hi, expand) + _mm(w_lo, expand)
        e_hi, e_lo = _split_bf16(e_cs)
        e_cs_x = _mm(e_hi, expand) + _mm(e_lo, expand)
        t_hi, t_lo = _split_bf16(jnp.broadcast_to(jnp.exp(tot), (F32_ROWS, LANES)))
        e_tot_x = (_mm(t_hi, expand) + _mm(t_lo, expand))[0:1]
        xw = (xs.astype(F32) * w_x).astype(BF16)
        y_groups = []
        for g in range(SSM_GROUPS):
            b_g = x_sc[e, :, SSM_D_INNER + g * SSM_D_STATE:SSM_D_INNER + (g + 1) * SSM_D_STATE]
            c_g = x_sc[e, :, SSM_D_INNER + SSM_BC + g * SSM_D_STATE:SSM_D_INNER + SSM_BC + (g + 1) * SSM_D_STATE]
            cb = _mm_nt(c_g, b_g)
            lanes = slice(g * group_lanes, (g + 1) * group_lanes)
            st = st_sc[e, g]
            y_off = _mm(c_g, st.astype(BF16)) * e_cs_x[:, lanes]
            st_sc[e, g] = e_tot_x[:, lanes] * st + _mm(b_g.astype(F32).T.astype(BF16), xw[:, lanes])
            ys = []
            for h in range(g * heads_per_group, (g + 1) * heads_per_group):
                decay = jnp.exp(jnp.where(causal, cs[:, h:h + 1] - cs_t[h:h + 1, :], NEG_BIG))
                ys.append(_mm((cb * decay * dt_t[h:h + 1, :]).astype(BF16), xs[:, h * HEAD_DIM:(h + 1) * HEAD_DIM]))
            y_groups.append(jnp.concatenate(ys, axis=-1) + y_off)
        return jnp.concatenate(y_groups, axis=-1)

    ys = [chunk_scan(e) for e in range(SSD_SEQS)]

    @pl.when(direction == 0)
    def _():
        for e in range(SSD_SEQS):
            y_sc[e, gid] = ys[e] + dsk_ref[...] * x_sc[e, :, :SSM_D_INNER].astype(F32)

    @pl.when(direction == 1)
    def _():
        res = []
        for e in range(SSD_SEQS):
            gated = (y_sc[e, gid] + ys[e]) * _silu(z_sc[e].astype(F32))
            res.append((_rms(gated) * nw_ref[...]).astype(BF16))

        @pl.when(is_ctx)
        def _():
            for e in range(SSD_SEQS):
                outc_ref[e] = res[e]

        @pl.when(jnp.logical_not(is_ctx))
        def _():
            for e in range(SSD_SEQS):
                outl_ref[e] = res[e]


def _ssd_mixer(xbc_l, dt_l, z_l, xbc_c, dt_c, z_c, a_log_p, d_skip_x, norm_w, tri, expand,
               n_batch, seq_len, ctx_len):
    T = SSM_CHUNK
    nlc = seq_len // T
    ncc = ctx_len // T
    n_chunks = nlc + ncc

    def ids(s):
        direction = s // n_chunks
        sp = s % n_chunks
        gid = jnp.where(direction == 0, sp, jnp.where(sp < ncc, ncc - 1 - sp, n_chunks + ncc - 1 - sp))
        return direction, gid, jnp.clip(gid, 0, ncc - 1), jnp.clip(gid - ncc, 0, nlc - 1)

    def lat_main(b, s):
        return (b, ids(s)[3], 0)

    def ctx_main(b, s):
        return (b, ids(s)[2], 0)

    def lat_dt(b, s):
        return (b, ids(s)[3], ids(s)[0])

    def ctx_dt(b, s):
        return (b, ids(s)[2], ids(s)[0])

    def lat_out(b, s):
        direction, gid, _, lc = ids(s)
        return (b, jnp.where(jnp.logical_and(direction == 1, gid >= ncc), lc, nlc - 1), 0)

    def ctx_out(b, s):
        direction, gid, cc, _ = ids(s)
        return (b, jnp.where(direction == 1, jnp.where(gid < ncc, cc, 0), ncc - 1), 0)

    def by_dir(b, s):
        return (ids(s)[0], 0, 0)

    const2 = lambda b, s: (0, 0)
    const3 = lambda b, s: (0, 0, 0)
    seqs = lambda t, n: t.reshape(n_batch, n, t.shape[-1])
    blk = lambda width, index_map: pl.BlockSpec((SSD_SEQS, T, width), index_map)
    out_l, out_c = pl.pallas_call(
        functools.partial(_ssd_kernel, n_chunks=n_chunks, n_ctx_chunks=ncc),
        out_shape=(jax.ShapeDtypeStruct((n_batch, seq_len, SSM_D_INNER), BF16),
                   jax.ShapeDtypeStruct((n_batch, ctx_len, SSM_D_INNER), BF16)),
        grid=(n_batch // SSD_SEQS, 2 * n_chunks),
        in_specs=[
            blk(SSM_XBC, lat_main),
            blk(SSM_XBC, ctx_main),
            blk(LANES, lat_dt),
            blk(LANES, ctx_dt),
            blk(SSM_D_INNER, lat_out),
            blk(SSM_D_INNER, ctx_out),
            pl.BlockSpec((None, 1, LANES), by_dir),
            pl.BlockSpec((1, SSM_D_INNER), const2),
            pl.BlockSpec((1, SSM_D_INNER), const2),
            pl.BlockSpec((2, T, T), const3),
            pl.BlockSpec((LANES, SSM_D_INNER), const2),
        ],
        out_specs=(blk(SSM_D_INNER, lat_out), blk(SSM_D_INNER, ctx_out)),
        scratch_shapes=[
            pltpu.VMEM((SSD_SEQS, T, SSM_XBC), BF16),
            pltpu.VMEM((SSD_SEQS, T, LANES), F32),
            pltpu.VMEM((SSD_SEQS, T, SSM_D_INNER), BF16),
            pltpu.VMEM((SSD_SEQS, n_chunks, T, SSM_D_INNER), F32),
            pltpu.VMEM((SSD_SEQS, SSM_GROUPS, SSM_D_STATE, SSM_D_INNER // SSM_GROUPS), F32),
        ],
        compiler_params=_params(("arbitrary", "arbitrary")),
        name="ssd_mixer",
    )(seqs(xbc_l, seq_len), seqs(xbc_c, ctx_len), seqs(dt_l, seq_len), seqs(dt_c, ctx_len),
      seqs(z_l, seq_len), seqs(z_c, ctx_len), a_log_p, d_skip_x, norm_w, tri, expand)
    return out_l.reshape(n_batch * seq_len, SSM_D_INNER), out_c.reshape(n_batch * ctx_len, SSM_D_INNER)


def _merge_kernel(h_ref, na_ref, ssm_ref, gqa_ref, gate_ref, g_ref, w1_ref, w2_ref, w3_ref, wo_ref, o_ref):
    gates = gate_ref[...]
    t = _sigmoid(gates[:, :D_MODEL].astype(F32)) * _mm(na_ref[...], w1_ref[...])
    t = t + _sigmoid(gates[:, D_MODEL:2 * D_MODEL].astype(F32)) * _mm(ssm_ref[...], w2_ref[...])
    t = t + _sigmoid(gates[:, 2 * D_MODEL:].astype(F32)) * _mm(gqa_ref[...], w3_ref[...])
    o_ref[...] = h_ref[...] + g_ref[...] * _mm(t.astype(BF16), wo_ref[...])


def _merge(h, o_na, o_ssm, o_gqa, gates, mod4, row_fn, w1, w2, w3, wo, tm):
    n_tok = h.shape[0]
    tok = lambda width: pl.BlockSpec((tm, width), lambda i: (i, 0))
    return pl.pallas_call(
        _merge_kernel,
        out_shape=jax.ShapeDtypeStruct((n_tok, D_MODEL), F32),
        grid=(n_tok // tm,),
        in_specs=[tok(D_MODEL), tok(NA_W), tok(SSM_D_INNER), tok(GQA_Q), tok(3 * D_MODEL),
                  _mod_spec(2, row_fn),
                  _resident((NA_W, D_MODEL)), _resident((SSM_D_INNER, D_MODEL)),
                  _resident((GQA_Q, D_MODEL)), _resident((D_MODEL, D_MODEL))],
        out_specs=tok(D_MODEL),
        compiler_params=_params(("arbitrary",)),
        name="branch_merge",
    )(h, o_na, o_ssm, o_gqa, gates, mod4, w1, w2, w3, wo)


def _ffn_kernel(h_ref, hp_ref, hn_ref, sh_ref, sc_ref, g_ref, nw_ref, wup_ref, cw_ref, wdn_ref, fw_ref,
                o_ref, hn_sc, u_sc, acc_sc, *, tm, tiles_per_seq, final_norm):
    j = pl.program_id(0) % tiles_per_seq
    seq_first = j == 0
    seq_last = j == tiles_per_seq - 1

    def norm_mod(x):
        return (_rms(x) * nw_ref[...]) * (1.0 + sc_ref[...]) + sh_ref[...]

    x = h_ref[...]
    hn_sc[0:tm] = norm_mod(x).astype(BF16)
    before = jnp.where(seq_first, 0.0, norm_mod(hp_ref[...]))
    after = jnp.where(seq_last, 0.0, norm_mod(hn_ref[...]))
    hn_sc[tm:] = jnp.concatenate([after, before], axis=0).astype(BF16)
    acc_sc[...] = jnp.zeros_like(acc_sc)

    def up(c):
        u_sc[c % 2] = _mm(hn_sc[...], wup_ref[c])

    up(0)
    for c in range(FFN_CHUNKS):
        if c + 1 < FFN_CHUNKS:
            up(c + 1)
        y = _conv3_rows(u_sc[c % 2], tm, cw_ref[c])
        act = (_silu(y[:, :FFN_TN]) * y[:, FFN_TN:]).astype(BF16)
        acc_sc[...] += _mm(act, wdn_ref[c])
    out = x + g_ref[...] * acc_sc[...]
    if final_norm:
        out = _rms(out) * fw_ref[...]
    o_ref[...] = out


def _conv_ffn(h, mod4, row_fn, norm_w, w_up_c, conv_c, w_down_c, final_w, tm, seq_len, final_norm):
    n_tok = h.shape[0]
    return pl.pallas_call(
        functools.partial(_ffn_kernel, tm=tm, tiles_per_seq=seq_len // tm, final_norm=final_norm),
        out_shape=jax.ShapeDtypeStruct((n_tok, D_MODEL), F32),
        grid=(n_tok // tm,),
        in_specs=[
            pl.BlockSpec((tm, D_MODEL), lambda i: (i, 0)),
            *_halo_specs(tm, n_tok),
            _mod_spec(3, row_fn),
            _mod_spec(4, row_fn),
            _mod_spec(5, row_fn),
            _resident((1, D_MODEL)),
            _resident((FFN_CHUNKS, D_MODEL, 2 * FFN_TN)),
            _resident((FFN_CHUNKS, F32_ROWS, 2 * FFN_TN)),
            _resident((FFN_CHUNKS, FFN_TN, D_MODEL)),
            _resident((1, D_MODEL)),
        ],
        out_specs=pl.BlockSpec((tm, D_MODEL), lambda i: (i, 0)),
        scratch_shapes=[pltpu.VMEM((tm + 2 * F32_ROWS, D_MODEL), BF16),
                        pltpu.VMEM((2, tm + 2 * F32_ROWS, 2 * FFN_TN), F32),
                        pltpu.VMEM((tm, D_MODEL), F32)],
        compiler_params=_params(("arbitrary",)),
        name="conv_ffn",
    )(h, h, h, mod4, mod4, mod4, norm_w, w_up_c, conv_c, w_down_c, final_w)


def _rope_tables(seq_len):
    n_freq = HEAD_DIM // 4
    inv_freq = ROPE_THETA ** (-jnp.arange(n_freq, dtype=F32) / n_freq)
    t = jnp.arange(seq_len, dtype=jnp.int32)
    row = (t // GRID_W).astype(F32)
    col = (t % GRID_W).astype(F32)
    ang = jnp.concatenate([row[:, None] * inv_freq, col[:, None] * inv_freq], axis=-1)
    cos = jnp.repeat(jnp.cos(ang), 2, axis=-1)
    sin = jnp.repeat(jnp.sin(ang), 2, axis=-1) * jnp.tile(jnp.asarray([-1.0, 1.0], F32), HEAD_DIM // 2)
    return jnp.tile(cos, (1, LANES // HEAD_DIM)), jnp.tile(sin, (1, LANES // HEAD_DIM))


def _na_bias_mask(rel_bias):
    qc = np.arange(GRID_W)[:, None]
    kc = np.arange(GRID_W)[None, :]
    win0 = np.clip(qc - NA_WIN_COLS // 2, 0, GRID_W - NA_WIN_COLS)
    ok = (kc >= win0) & (kc < win0 + NA_WIN_COLS)
    dcol = np.clip(kc - qc + NA_WIN_COLS - 1, 0, 2 * NA_WIN_COLS - 2)
    per_drow = jnp.where(jnp.asarray(ok)[None, None], rel_bias.astype(F32)[:, :, dcol], NEG_BIG)
    n_heads = rel_bias.shape[0]
    masked = jnp.full((n_heads, GRID_W, GRID_W), NEG_BIG, F32)
    half = NA_WIN_ROWS // 2
    classes = []
    for cls in range(3):
        rows = []
        for j in range(NA_ROW_GROUP):
            first_slab_row = (0, j, NA_SLAB_ROWS - NA_WIN_ROWS)[cls]
            drow0 = (NA_WIN_ROWS - 1 - j, NA_WIN_ROWS - 1 - half, NA_WIN_ROWS - 1 - half - j)[cls]
            blocks = [per_drow[:, drow0 + sr - first_slab_row] if 0 <= sr - first_slab_row < NA_WIN_ROWS else masked
                      for sr in range(NA_SLAB_ROWS)]
            rows.append(jnp.concatenate(blocks, axis=-1))
        classes.append(jnp.concatenate(rows, axis=1))
    table = jnp.stack(classes, axis=1)
    table = table.reshape(n_heads // 2, 2, 3, NA_ROW_GROUP * GRID_W, NA_SLAB_ROWS * GRID_W)
    return jnp.moveaxis(table, 1, 2).reshape(n_heads // 2, 3, 2 * NA_ROW_GROUP * GRID_W, NA_SLAB_ROWS * GRID_W)


def _pack_w_in(w_in):
    sizes = [3 * NA_W, SSM_D_INNER, SSM_XBC, 2 * SSM_HEADS, GQA_Q, GQA_KV, GQA_KV]
    na, z, xbc, dt, gq, gk, gv, gate = jnp.split(w_in, [int(i) for i in np.cumsum(sizes)], axis=-1)
    na = jnp.concatenate([na[:, :NA_W] * HEAD_DIM ** -0.5, na[:, NA_W:]], axis=-1)
    w_main = jnp.concatenate([na, z, xbc, gq, gk, gv, gate], axis=-1).astype(BF16)
    dt_pad = jnp.zeros((D_MODEL, DT_COLS), F32)
    dt_pad = dt_pad.at[:, :SSM_HEADS].set(dt[:, :SSM_HEADS]).at[:, LANES:LANES + SSM_HEADS].set(dt[:, SSM_HEADS:])
    dt_hi, dt_lo = _split_bf16(dt_pad)
    return w_main, dt_hi, dt_lo


def _pad_heads(v):
    return jnp.zeros((2, 1, LANES), F32).at[:, 0, :SSM_HEADS].set(v.astype(F32))


def _pack_ffn(w_up, conv_w, conv_b, w_down):
    def chunked(t):
        a, b = t[..., :FFN_HIDDEN], t[..., FFN_HIDDEN:]
        lead = t.shape[:-1]
        return jnp.concatenate([a.reshape(lead + (FFN_CHUNKS, FFN_TN)), b.reshape(lead + (FFN_CHUNKS, FFN_TN))], axis=-1)

    w_up_c = jnp.moveaxis(chunked(w_up), 1, 0).astype(BF16)
    taps = jnp.concatenate([conv_w, conv_b[None], jnp.zeros((F32_ROWS - 4, 2 * FFN_HIDDEN), F32)], axis=0)
    conv_c = jnp.moveaxis(chunked(taps), 1, 0)
    w_down_c = w_down.reshape(FFN_CHUNKS, FFN_TN, D_MODEL).astype(BF16)
    return w_up_c, conv_c, w_down_c


def _scan_masks():
    t = np.arange(SSM_CHUNK)
    tril = (t[:, None] >= t[None, :]).astype(np.float32)
    return jnp.asarray(np.stack([tril, tril.T]), BF16)


def kernel(x, c, ctx, c_ctx, w_mod, b_mod, norm1_w, norm2_w, w_in, na_rel_bias, ssm_conv_w, ssm_conv_b, ssm_a_log, ssm_dt_bias, ssm_d, ssm_norm_w, q_norm_w, k_norm_w, w_out_na, w_out_ssm, w_out_gqa, w_o, ffn_w_up, ffn_conv_w, ffn_conv_b, ffn_w_down, final_norm_w):
    n_batch, seq_len, _ = x.shape
    ctx_len = ctx.shape[1]
    depth = w_mod.shape[0]
    assert seq_len % (GRID_W * NA_ROW_GROUP) == 0 and seq_len // GRID_W >= NA_SLAB_ROWS
    assert n_batch <= CTX_MOD_ROW and n_batch % SSD_SEQS == 0 and seq_len % 512 == 0 and ctx_len % 256 == 0
    tm_lat, tm_ctx = 512, 256

    cc = jnp.zeros((MOD_ROWS, D_MODEL), F32).at[:n_batch].set(c).at[CTX_MOD_ROW].set(c_ctx)
    mods = _mod_vectors(cc, w_mod, b_mod)
    lat_row = lambda i: i // (seq_len // tm_lat)
    ctx_row = lambda i: CTX_MOD_ROW

    cos_l, sin_l = _rope_tables(seq_len)
    cos_c, sin_c = jnp.ones((ctx_len, LANES), F32), jnp.zeros((ctx_len, LANES), F32)
    head_of_lane = np.arange(LANES) // HEAD_DIM
    same_head = head_of_lane[:, None] == head_of_lane[None, :]
    seg_ones = jnp.asarray(np.concatenate([same_head, same_head], axis=0), BF16)
    tri = _scan_masks()
    lane_head = np.arange(SSM_D_INNER) // HEAD_DIM
    expand = jnp.asarray(np.arange(LANES)[:, None] == lane_head[None, :], BF16)

    h_lat = x.reshape(n_batch * seq_len, D_MODEL)
    h_ctx = ctx.reshape(n_batch * ctx_len, D_MODEL)
    row = lambda v: v.reshape(1, -1).astype(F32)
    for layer in range(depth):
        need_ctx = layer < depth - 1
        last = layer == depth - 1
        mod4 = mods[layer].reshape(MOD_ROWS, N_MOD, 1, D_MODEL)
        w_main, w_dt_hi, w_dt_lo = _pack_w_in(w_in[layer])
        qw = row(jnp.tile(q_norm_w[layer], LANES // HEAD_DIM))
        kw = row(jnp.tile(k_norm_w[layer], LANES // HEAD_DIM))
        conv_taps = jnp.concatenate(
            [ssm_conv_w[layer], ssm_conv_b[layer][None], jnp.zeros((F32_ROWS - 4, SSM_XBC), F32)], axis=0)
        dt_bias = jnp.zeros((1, DT_COLS), F32)
        dt_bias = dt_bias.at[0, :SSM_HEADS].set(ssm_dt_bias[layer, 0]).at[0, LANES:LANES + SSM_HEADS].set(
            ssm_dt_bias[layer, 1])
        proj = functools.partial(_in_proj, mod4=mod4, norm_w=row(norm1_w[layer]), qw=qw, kw=kw, seg_ones=seg_ones,
                                 conv_taps=conv_taps, dt_bias=dt_bias,
                                 w_main=w_main, w_dt_hi=w_dt_hi, w_dt_lo=w_dt_lo)
        na_l, z_l, xbc_l, dt_l, q_l, k_l, v_l, gate_l = proj(
            h_lat, row_fn=lat_row, cos=cos_l, sin=sin_l, tm=tm_lat, seq_len=seq_len)
        na_c, z_c, xbc_c, dt_c, q_c, k_c, v_c, gate_c = proj(
            h_ctx, row_fn=ctx_row, cos=cos_c, sin=sin_c, tm=tm_ctx, seq_len=ctx_len)

        o_na_l = _neighbourhood_attention(na_l, na_c, _na_bias_mask(na_rel_bias[layer]), n_batch, seq_len, ctx_len)
        o_ssm_l, o_ssm_c = _ssd_mixer(
            xbc_l, dt_l, z_l, xbc_c, dt_c, z_c, _pad_heads(ssm_a_log[layer]),
            row(jnp.repeat(ssm_d[layer], HEAD_DIM)), row(ssm_norm_w[layer]), tri, expand,
            n_batch, seq_len, ctx_len)
        per_batch = lambda t, n: t.reshape(GQA_KV_HEADS, n_batch, n, HEAD_DIM)
        with_ones = lambda v: jnp.concatenate(
            [v, jnp.ones(v.shape[:-1] + (1,), BF16), jnp.zeros(v.shape[:-1] + (LANES - HEAD_DIM - 1,), BF16)], axis=-1)
        k_ctx, v_ctx = per_batch(k_c, ctx_len), with_ones(per_batch(v_c, ctx_len))
        k_all = jnp.concatenate([k_ctx, per_batch(k_l, seq_len)], axis=2)
        v_all = jnp.concatenate([v_ctx, with_ones(per_batch(v_l, seq_len))], axis=2)
        o_gqa_l = _gqa_attention(q_l, k_all, v_all, n_batch, seq_len, tq=256)

        w1, w2, w3, wo = (w.astype(BF16) for w in (w_out_na[layer], w_out_ssm[layer], w_out_gqa[layer], w_o[layer]))
        ffn_w = _pack_ffn(ffn_w_up[layer], ffn_conv_w[layer], ffn_conv_b[layer], ffn_w_down[layer])
        h_lat = _merge(h_lat, o_na_l, o_ssm_l, o_gqa_l, gate_l, mod4, lat_row, w1, w2, w3, wo, tm_lat)
        h_lat = _conv_ffn(h_lat, mod4, lat_row, row(norm2_w[layer]), *ffn_w, row(final_norm_w),
                          tm=tm_lat, seq_len=seq_len, final_norm=last)
        if need_ctx:
            o_na_c = _context_mha(na_c, n_batch, ctx_len)
            o_gqa_c = _gqa_attention(q_c, k_ctx, v_ctx, n_batch, ctx_len, tq=ctx_len)
            h_ctx = _merge(h_ctx, o_na_c, o_ssm_c, o_gqa_c, gate_c, mod4, ctx_row, w1, w2, w3, wo, tm_ctx)
            h_ctx = _conv_ffn(h_ctx, mod4, ctx_row, row(norm2_w[layer]), *ffn_w, row(final_norm_w),
                              tm=tm_ctx, seq_len=ctx_len, final_norm=False)
    return h_lat.reshape(n_batch, seq_len, D_MODEL)
```

```python
import functools

import numpy as np
import jax
import jax.numpy as jnp
from jax import lax
from jax.experimental import pallas as pl
from jax.experimental.pallas import tpu as pltpu

F32 = jnp.float32
BF16 = jnp.bfloat16

D_MODEL = 1024
GRID_W = 64
EPS = 1e-6
HEAD_DIM = 64
NA_HEADS = 8
NA_W = NA_HEADS * HEAD_DIM
NA_WIN_ROWS = 8
NA_WIN_COLS = 16
NA_ROW_GROUP = 4
NA_SLAB_ROWS = 12
SSM_HEADS = 16
SSM_D_INNER = SSM_HEADS * HEAD_DIM
SSM_GROUPS = 2
SSM_D_STATE = 64
SSM_BC = SSM_GROUPS * SSM_D_STATE
SSM_XBC = SSM_D_INNER + 2 * SSM_BC
SSM_CHUNK = 128
SSD_SEQS = 2
GQA_HEADS = 8
GQA_KV_HEADS = 2
GQA_GROUP = GQA_HEADS // GQA_KV_HEADS
GQA_Q = GQA_HEADS * HEAD_DIM
GQA_KV = GQA_KV_HEADS * HEAD_DIM
GQA_BK = 256
GQA_ROW_CHUNK = 128
ROPE_THETA = 10000.0
FFN_HIDDEN = 2816
N_MOD = 6
MOD_ROWS = 16
CTX_MOD_ROW = 8
NEG_BIG = -1e30

LANES = 128
BF16_ROWS = 16
F32_ROWS = 8
VMEM_LIMIT = 56 * 1024 * 1024

SEG_NA = (0, 3 * NA_W)
SEG_Z = (SEG_NA[1], SEG_NA[1] + SSM_D_INNER)
SEG_XBC = (SEG_Z[1], SEG_Z[1] + SSM_XBC)
SEG_GQ = (SEG_XBC[1], SEG_XBC[1] + GQA_Q)
SEG_GK = (SEG_GQ[1], SEG_GQ[1] + GQA_KV)
SEG_GV = (SEG_GK[1], SEG_GK[1] + GQA_KV)
SEG_GATE = (SEG_GV[1], SEG_GV[1] + 3 * D_MODEL)
W_MAIN_COLS = SEG_GATE[1]
DT_COLS = 2 * LANES
DOT_COLS = 512

FFN_TN = 256
FFN_CHUNKS = FFN_HIDDEN // FFN_TN


def _mm(a, b):
    return jnp.dot(a, b, preferred_element_type=F32)


def _mm_nt(a, b):
    return lax.dot_general(a, b, (((1,), (1,)), ((), ())), preferred_element_type=F32)


def _split_bf16(x):
    hi = x.astype(BF16)
    lo = (x - hi.astype(F32)).astype(BF16)
    return hi, lo


def _sigmoid(x):
    return 1.0 / (1.0 + jnp.exp(-x))


def _silu(x):
    return x * _sigmoid(x)


def _softplus(x):
    return jnp.maximum(x, 0.0) + jnp.log1p(jnp.exp(-jnp.abs(x)))


def _rms(x):
    return x * lax.rsqrt(jnp.mean(x * x, axis=-1, keepdims=True) + EPS)


def _resident(shape):
    nd = len(shape)
    return pl.BlockSpec(shape, lambda *_: (0,) * nd, pipeline_mode=pl.Buffered(1))


def _params(sem):
    return pltpu.CompilerParams(dimension_semantics=sem, vmem_limit_bytes=VMEM_LIMIT)


def _mod_kernel(c_ref, w_ref, b_ref, o_ref):
    x_hi, x_lo = _split_bf16(_silu(c_ref[...]))
    w_hi, w_lo = _split_bf16(w_ref[...])
    o_ref[...] = _mm(x_hi, w_hi) + _mm(x_lo, w_hi) + _mm(x_hi, w_lo) + b_ref[...]


def _mod_vectors(cc, w_mod, b_mod):
    n_layers = w_mod.shape[0]
    tn = D_MODEL
    return pl.pallas_call(
        _mod_kernel,
        out_shape=jax.ShapeDtypeStruct((n_layers, MOD_ROWS, N_MOD * D_MODEL), F32),
        grid=(n_layers, N_MOD),
        in_specs=[
            pl.BlockSpec((MOD_ROWS, D_MODEL), lambda l, j: (0, 0)),
            pl.BlockSpec((None, D_MODEL, tn), lambda l, j: (l, 0, j)),
            pl.BlockSpec((None, 1, tn), lambda l, j: (l, 0, j)),
        ],
        out_specs=pl.BlockSpec((None, MOD_ROWS, tn), lambda l, j: (l, 0, j)),
        compiler_params=_params(("arbitrary", "arbitrary")),
        name="mod_vectors",
    )(cc, w_mod, b_mod.reshape(n_layers, 1, N_MOD * D_MODEL))


def _mod_spec(which, row_fn):
    return pl.BlockSpec((None, None, 1, D_MODEL), lambda i: (row_fn(i), which, 0, 0))


def _qk_norm_rope(x, w, cos, sin, seg_ones):
    ss = _mm(jnp.concatenate(_split_bf16(x * x), axis=1), seg_ones)
    xn = x * lax.rsqrt(ss * (1.0 / HEAD_DIM) + EPS) * w
    lane = lax.broadcasted_iota(jnp.int32, xn.shape, 1)
    partner = jnp.where((lane & 1) == 0, pltpu.roll(xn, LANES - 1, 1), pltpu.roll(xn, 1, 1))
    return xn * cos + partner * sin


def _conv3_rows(u, tm, taps):
    n = u.shape[0]
    return (taps[0:1] * pltpu.roll(u, 1, 0)[0:tm] + taps[1:2] * u[0:tm]
            + taps[2:3] * pltpu.roll(u, n - 1, 0)[0:tm] + taps[3:4])


def _inproj_kernel(h_ref, hp_ref, hn_ref, sh_ref, sc_ref, nw_ref, cos_ref, sin_ref, qw_ref, kw_ref, seg_ref,
                   cw_ref, dtb_ref, w_ref, wdh_ref, wdl_ref,
                   na_ref, z_ref, xbc_ref, dt_ref, q_ref, k_ref, v_ref, gate_ref, y_sc, *, tm, tiles_per_seq):
    j = pl.program_id(0) % tiles_per_seq

    def norm_mod(x):
        return (_rms(x) * nw_ref[...]) * (1.0 + sc_ref[...]) + sh_ref[...]

    y = norm_mod(h_ref[...])
    y_hi, y_lo = _split_bf16(y)
    before = jnp.where(j == 0, 0.0, norm_mod(hp_ref[...]))
    after = jnp.where(j == tiles_per_seq - 1, 0.0, norm_mod(hn_ref[...]))
    y_sc[0:tm] = y_hi
    y_sc[tm:] = jnp.concatenate([after, before], axis=0).astype(BF16)

    def project(out_ref, seg):
        for c0 in range(seg[0], seg[1], DOT_COLS):
            c1 = min(c0 + DOT_COLS, seg[1])
            out_ref[:, c0 - seg[0]:c1 - seg[0]] = _mm(y_hi, w_ref[:, c0:c1]).astype(out_ref.dtype)

    for c0 in range(SEG_XBC[0], SEG_XBC[1], DOT_COLS):
        c1 = min(c0 + DOT_COLS, SEG_XBC[1])
        cols = slice(c0 - SEG_XBC[0], c1 - SEG_XBC[0])
        xbc_ref[:, cols] = _silu(_conv3_rows(_mm(y_sc[...], w_ref[:, c0:c1]), tm, cw_ref[:, cols])).astype(BF16)
    dt_raw = _mm(y_hi, wdh_ref[...]) + _mm(y_lo, wdh_ref[...]) + _mm(y_hi, wdl_ref[...])
    dt_ref[...] = _softplus(dt_raw + dtb_ref[...])

    cos = cos_ref[...]
    sin = sin_ref[...]
    seg_ones = seg_ref[...]
    for pair in range(GQA_Q // LANES):
        c0 = SEG_GQ[0] + pair * LANES
        xr = _qk_norm_rope(_mm(y_hi, w_ref[:, c0:c0 + LANES]), qw_ref[...], cos, sin, seg_ones)
        xr = (xr * HEAD_DIM ** -0.5).astype(BF16)
        q_ref[2 * pair] = xr[:, :HEAD_DIM]
        q_ref[2 * pair + 1] = xr[:, HEAD_DIM:]
    kr = _qk_norm_rope(_mm(y_hi, w_ref[:, SEG_GK[0]:SEG_GK[1]]), kw_ref[...], cos, sin, seg_ones).astype(BF16)
    k_ref[0] = kr[:, :HEAD_DIM]
    k_ref[1] = kr[:, HEAD_DIM:]
    vv = _mm(y_hi, w_ref[:, SEG_GV[0]:SEG_GV[1]]).astype(BF16)
    v_ref[0] = vv[:, :HEAD_DIM]
    v_ref[1] = vv[:, HEAD_DIM:]
    project(na_ref, SEG_NA)
    project(z_ref, SEG_Z)
    project(gate_ref, SEG_GATE)


def _halo_specs(tm, n_tok):
    per_tile = tm // F32_ROWS
    last = n_tok // F32_ROWS - 1
    return [pl.BlockSpec((F32_ROWS, D_MODEL), lambda i: (jnp.maximum(i * per_tile - 1, 0), 0)),
            pl.BlockSpec((F32_ROWS, D_MODEL), lambda i: (jnp.minimum((i + 1) * per_tile, last), 0))]


def _in_proj(h, mod4, row_fn, norm_w, cos, sin, qw, kw, seg_ones, conv_taps, dt_bias, w_main, w_dt_hi, w_dt_lo,
             tm, seq_len):
    n_tok = h.shape[0]
    pos_tiles = seq_len // tm
    tok = lambda width: pl.BlockSpec((tm, width), lambda i: (i, 0))
    hm = lambda nh: pl.BlockSpec((nh, tm, HEAD_DIM), lambda i: (0, i, 0))
    outs = pl.pallas_call(
        functools.partial(_inproj_kernel, tm=tm, tiles_per_seq=pos_tiles),
        out_shape=(
            jax.ShapeDtypeStruct((n_tok, 3 * NA_W), BF16),
            jax.ShapeDtypeStruct((n_tok, SSM_D_INNER), BF16),
            jax.ShapeDtypeStruct((n_tok, SSM_XBC), BF16),
            jax.ShapeDtypeStruct((n_tok, DT_COLS), F32),
            jax.ShapeDtypeStruct((GQA_HEADS, n_tok, HEAD_DIM), BF16),
            jax.ShapeDtypeStruct((GQA_KV_HEADS, n_tok, HEAD_DIM), BF16),
            jax.ShapeDtypeStruct((GQA_KV_HEADS, n_tok, HEAD_DIM), BF16),
            jax.ShapeDtypeStruct((n_tok, 3 * D_MODEL), BF16),
        ),
        grid=(n_tok // tm,),
        in_specs=[
            tok(D_MODEL),
            *_halo_specs(tm, n_tok),
            _mod_spec(0, row_fn),
            _mod_spec(1, row_fn),
            _resident((1, D_MODEL)),
            pl.BlockSpec((tm, LANES), lambda i: (i % pos_tiles, 0)),
            pl.BlockSpec((tm, LANES), lambda i: (i % pos_tiles, 0)),
            _resident((1, LANES)),
            _resident((1, LANES)),
            _resident((2 * LANES, LANES)),
            _resident((F32_ROWS, SSM_XBC)),
            _resident((1, DT_COLS)),
            _resident((D_MODEL, W_MAIN_COLS)),
            _resident((D_MODEL, DT_COLS)),
            _resident((D_MODEL, DT_COLS)),
        ],
        out_specs=(tok(3 * NA_W), tok(SSM_D_INNER), tok(SSM_XBC), tok(DT_COLS),
                   hm(GQA_HEADS), hm(GQA_KV_HEADS), hm(GQA_KV_HEADS), tok(3 * D_MODEL)),
        scratch_shapes=[pltpu.VMEM((tm + 2 * F32_ROWS, D_MODEL), BF16)],
        compiler_params=_params(("arbitrary",)),
        name="in_proj",
    )(h, h, h, mod4, mod4, norm_w, cos, sin, qw, kw, seg_ones, conv_taps, dt_bias, w_main, w_dt_hi, w_dt_lo)
    return outs


def _softmax_pv(s_list, v_list):
    m = s_list[0].max(axis=-1, keepdims=True)
    for s in s_list[1:]:
        m = jnp.maximum(m, s.max(axis=-1, keepdims=True))
    den = 0.0
    out = 0.0
    for s, v in zip(s_list, v_list):
        p = jnp.exp(s - m)
        den = den + p.sum(axis=-1, keepdims=True)
        out = out + _mm(p.astype(BF16), v)
    return out / den


def _na_kernel(q_ref, k_ref, v_ref, kc_ref, vc_ref, bm_ref, o_ref, *, n_rows):
    n_groups = n_rows // NA_ROW_GROUP
    gq = NA_ROW_GROUP * GRID_W
    slab = NA_SLAB_ROWS * GRID_W
    kc = kc_ref[...]
    vc = vc_ref[...]
    head0 = lax.broadcasted_iota(jnp.int32, (1, LANES), 1) < HEAD_DIM

    def group(i, carry):
        r0 = i * NA_ROW_GROUP
        start = jnp.clip(r0 - NA_WIN_ROWS // 2, 0, n_rows - NA_SLAB_ROWS)
        cls = jnp.where(i == 0, 0, jnp.where(i == n_groups - 1, 2, 1))
        q2 = q_ref[pl.ds(pl.multiple_of(r0 * GRID_W, gq), gq), :]
        ks = k_ref[pl.ds(pl.multiple_of(start * GRID_W, GRID_W), slab), :]
        vs = v_ref[pl.ds(pl.multiple_of(start * GRID_W, GRID_W), slab), :]
        zero = jnp.zeros_like(q2)
        qs = jnp.concatenate([jnp.where(head0, q2, zero), jnp.where(head0, zero, q2)], axis=0)
        s_win = _mm_nt(qs, ks) + bm_ref[cls]
        s_ctx = _mm_nt(qs, kc)
        o = _softmax_pv([s_win, s_ctx], [vs, vc])
        o_ref[pl.ds(pl.multiple_of(r0 * GRID_W, gq), gq), :] = jnp.where(head0, o[:gq], o[gq:]).astype(BF16)
        return carry

    lax.fori_loop(0, n_groups, group, 0, unroll=8)


def _neighbourhood_attention(na_l, na_c, bias_mask, n_batch, seq_len, ctx_len):
    n_rows = seq_len // GRID_W
    n_pairs = NA_W // LANES
    lat = lambda part: pl.BlockSpec((seq_len, LANES), lambda b, p: (b, part * n_pairs + p))
    ctx = lambda part: pl.BlockSpec((ctx_len, LANES), lambda b, p: (b, part * n_pairs + p))
    return pl.pallas_call(
        functools.partial(_na_kernel, n_rows=n_rows),
        out_shape=jax.ShapeDtypeStruct((n_batch * seq_len, NA_W), BF16),
        grid=(n_batch, n_pairs),
        in_specs=[lat(0), lat(1), lat(2), ctx(1), ctx(2),
                  pl.BlockSpec((None, 3, 2 * NA_ROW_GROUP * GRID_W, NA_SLAB_ROWS * GRID_W),
                               lambda b, p: (p, 0, 0, 0))],
        out_specs=pl.BlockSpec((seq_len, LANES), lambda b, p: (b, p)),
        compiler_params=_params(("arbitrary", "arbitrary")),
        name="neighbourhood_attention",
    )(na_l, na_l, na_l, na_c, na_c, bias_mask)


def _ctx_mha_kernel(q_ref, k_ref, v_ref, o_ref):
    q2 = q_ref[...]
    k2 = k_ref[...]
    v2 = v_ref[...]
    head0 = lax.broadcasted_iota(jnp.int32, (1, LANES), 1) < HEAD_DIM
    outs = []
    for hh in range(2):
        keep = head0 if hh == 0 else jnp.logical_not(head0)
        qm = jnp.where(keep, q2, jnp.zeros_like(q2))
        outs.append(_softmax_pv([_mm_nt(qm, k2)], [v2]))
    o_ref[...] = jnp.where(head0, outs[0], outs[1]).astype(BF16)


def _context_mha(na_c, n_batch, ctx_len):
    n_pairs = NA_W // LANES
    part = lambda which: pl.BlockSpec((ctx_len, LANES), lambda b, p: (b, which * n_pairs + p))
    return pl.pallas_call(
        _ctx_mha_kernel,
        out_shape=jax.ShapeDtypeStruct((n_batch * ctx_len, NA_W), BF16),
        grid=(n_batch, n_pairs),
        in_specs=[part(0), part(1), part(2)],
        out_specs=pl.BlockSpec((ctx_len, LANES), lambda b, p: (b, p)),
        compiler_params=_params(("arbitrary", "arbitrary")),
        name="context_mha",
    )(na_c, na_c, na_c)


def _gqa_kernel(q_ref, k_ref, v_ref, o_ref, s_sc, p_sc, m_sc, a_sc, acc_sc, *, tq, blocks):
    rows = GQA_GROUP * tq
    q = q_ref[...].reshape(rows, HEAD_DIM)
    m_sc[...] = jnp.full_like(m_sc, NEG_BIG)
    acc_sc[...] = jnp.zeros_like(acc_sc)

    def scores(blk, slot):
        k0, size = blk
        s_sc[slot, :, 0:size] = _mm_nt(q, k_ref[k0:k0 + size, :])

    def absorb(blk, slot):
        k0, size = blk
        for c0 in range(0, rows, GQA_ROW_CHUNK):
            rs = slice(c0, c0 + GQA_ROW_CHUNK)
            tiles = [s_sc[slot, rs, t * LANES:(t + 1) * LANES] for t in range(size // LANES)]
            m_old = m_sc[rs, :]
            m_new = jnp.maximum(m_old, functools.reduce(jnp.maximum, tiles).max(axis=-1, keepdims=True))
            a_sc[rs, :] = jnp.exp(m_old - m_new)
            m_sc[rs, :] = m_new
            for t, s in enumerate(tiles):
                p_sc[rs, t * LANES:(t + 1) * LANES] = jnp.exp(s - m_new).astype(BF16)
        acc_sc[...] = a_sc[...] * acc_sc[...] + _mm(p_sc[:, 0:size], v_ref[k0:k0 + size, :])

    scores(blocks[0], 0)
    for j, blk in enumerate(blocks):
        if j + 1 < len(blocks):
            scores(blocks[j + 1], (j + 1) % 2)
        absorb(blk, j % 2)
    acc = acc_sc[...]
    o = acc[:, :HEAD_DIM] / acc[:, HEAD_DIM:HEAD_DIM + 1]
    o_ref[...] = jnp.concatenate([o[g * tq:(g + 1) * tq] for g in range(GQA_GROUP)], axis=-1).astype(BF16)


def _gqa_attention(q_hm, k_all, v_aug, n_batch, q_len, tq):
    q_tiles = q_len // tq
    nk = k_all.shape[2]
    assert nk % (2 * LANES) == 0
    first = nk % GQA_BK
    blocks = ([(0, first)] if first else []) + [(k0, GQA_BK) for k0 in range(first, nk, GQA_BK)]
    rows = GQA_GROUP * tq
    return pl.pallas_call(
        functools.partial(_gqa_kernel, tq=tq, blocks=tuple(blocks)),
        out_shape=jax.ShapeDtypeStruct((n_batch * q_len, GQA_Q), BF16),
        grid=(n_batch, GQA_KV_HEADS, q_tiles),
        in_specs=[
            pl.BlockSpec((GQA_GROUP, tq, HEAD_DIM), lambda b, g, i: (g, b * q_tiles + i, 0)),
            pl.BlockSpec((None, None, nk, HEAD_DIM), lambda b, g, i: (g, b, 0, 0)),
            pl.BlockSpec((None, None, nk, LANES), lambda b, g, i: (g, b, 0, 0)),
        ],
        out_specs=pl.BlockSpec((tq, GQA_GROUP * HEAD_DIM), lambda b, g, i: (b * q_tiles + i, g)),
        scratch_shapes=[
            pltpu.VMEM((2, rows, GQA_BK), F32),
            pltpu.VMEM((rows, GQA_BK), BF16),
            pltpu.VMEM((rows, LANES), F32),
            pltpu.VMEM((rows, LANES), F32),
            pltpu.VMEM((rows, LANES), F32),
        ],
        compiler_params=_params(("arbitrary", "arbitrary", "arbitrary")),
        name="gqa_attention",
    )(q_hm, k_all, v_aug)


def _ssd_kernel(xl_ref, xc_ref, dtl_ref, dtc_ref, zl_ref, zc_ref, alog_ref, dsk_ref, nw_ref, tri_ref, exp_ref,
                outl_ref, outc_ref, x_sc, dt_sc, z_sc, y_sc, st_sc, *, n_chunks, n_ctx_chunks):
    s = pl.program_id(1)
    direction = s // n_chunks
    sp = s % n_chunks
    gid = jnp.where(direction == 0, sp,
                    jnp.where(sp < n_ctx_chunks, n_ctx_chunks - 1 - sp, n_chunks + n_ctx_chunks - 1 - sp))
    is_ctx = gid < n_ctx_chunks

    def stage(x_ref, dt_ref, z_ref):
        x_sc[...] = x_ref[...]
        dt_sc[...] = dt_ref[...]
        z_sc[...] = z_ref[...]

    @pl.when(is_ctx)
    def _():
        stage(xc_ref, dtc_ref, zc_ref)

    @pl.when(jnp.logical_not(is_ctx))
    def _():
        stage(xl_ref, dtl_ref, zl_ref)

    @pl.when(sp == 0)
    def _():
        st_sc[...] = jnp.zeros_like(st_sc)

    tri = tri_ref[direction]
    causal = tri > 0
    neg_a = -jnp.exp(alog_ref[...])
    expand = exp_ref[...]
    heads_per_group = SSM_HEADS // SSM_GROUPS
    group_lanes = heads_per_group * HEAD_DIM

    def chunk_scan(e):
        xs = x_sc[e, :, :SSM_D_INNER]
        dt = dt_sc[e]
        da = dt * neg_a
        da_hi, da_lo = _split_bf16(da)
        cs = _mm(tri, da_hi) + _mm(tri, da_lo)
        tot = da.sum(axis=0, keepdims=True)
        w = dt * jnp.exp(tot - cs)
        e_cs = jnp.exp(cs)
        cs_t = cs.T
        dt_t = dt.T
        w_hi, w_lo = _split_bf16(w)
        w_x = _mm(w_hi, expand) + _mm(w_lo, expand)
        e_hi, e_lo = _split_bf16(e_cs)
        e_cs_x = _mm(e_hi, expand) + _mm(e_lo, expand)
        t_hi, t_lo = _split_bf16(jnp.broadcast_to(jnp.exp(tot), (F32_ROWS, LANES)))
        e_tot_x = (_mm(t_hi, expand) + _mm(t_lo, expand))[0:1]
        xw = (xs.astype(F32) * w_x).astype(BF16)
        y_groups = []
        for g in range(SSM_GROUPS):
            b_g = x_sc[e, :, SSM_D_INNER + g * SSM_D_STATE:SSM_D_INNER + (g + 1) * SSM_D_STATE]
            c_g = x_sc[e, :, SSM_D_INNER + SSM_BC + g * SSM_D_STATE:SSM_D_INNER + SSM_BC + (g + 1) * SSM_D_STATE]
            cb = _mm_nt(c_g, b_g)
            lanes = slice(g * group_lanes, (g + 1) * group_lanes)
            st = st_sc[e, g]
            y_off = _mm(c_g, st.astype(BF16)) * e_cs_x[:, lanes]
            st_sc[e, g] = e_tot_x[:, lanes] * st + _mm(b_g.astype(F32).T.astype(BF16), xw[:, lanes])
            ys = []
            for h in range(g * heads_per_group, (g + 1) * heads_per_group):
                decay = jnp.exp(jnp.where(causal, cs[:, h:h + 1] - cs_t[h:h + 1, :], NEG_BIG))
                ys.append(_mm((cb * decay * dt_t[h:h + 1, :]).astype(BF16), xs[:, h * HEAD_DIM:(h + 1) * HEAD_DIM]))
            y_groups.append(jnp.concatenate(ys, axis=-1) + y_off)
        return jnp.concatenate(y_groups, axis=-1)

    ys = [chunk_scan(e) for e in range(SSD_SEQS)]

    @pl.when(direction == 0)
    def _():
        for e in range(SSD_SEQS):
            y_sc[e, gid] = ys[e] + dsk_ref[...] * x_sc[e, :, :SSM_D_INNER].astype(F32)

    @pl.when(direction == 1)
    def _():
        res = []
        for e in range(SSD_SEQS):
            gated = (y_sc[e, gid] + ys[e]) * _silu(z_sc[e].astype(F32))
            res.append((_rms(gated) * nw_ref[...]).astype(BF16))

        @pl.when(is_ctx)
        def _():
            for e in range(SSD_SEQS):
                outc_ref[e] = res[e]

        @pl.when(jnp.logical_not(is_ctx))
        def _():
            for e in range(SSD_SEQS):
                outl_ref[e] = res[e]


def _ssd_mixer(xbc_l, dt_l, z_l, xbc_c, dt_c, z_c, a_log_p, d_skip_x, norm_w, tri, expand,
               n_batch, seq_len, ctx_len):
    T = SSM_CHUNK
    nlc = seq_len // T
    ncc = ctx_len // T
    n_chunks = nlc + ncc

    def ids(s):
        direction = s // n_chunks
        sp = s % n_chunks
        gid = jnp.where(direction == 0, sp, jnp.where(sp < ncc, ncc - 1 - sp, n_chunks + ncc - 1 - sp))
        return direction, gid, jnp.clip(gid, 0, ncc - 1), jnp.clip(gid - ncc, 0, nlc - 1)

    def lat_main(b, s):
        return (b, ids(s)[3], 0)

    def ctx_main(b, s):
        return (b, ids(s)[2], 0)

    def lat_dt(b, s):
        return (b, ids(s)[3], ids(s)[0])

    def ctx_dt(b, s):
        return (b, ids(s)[2], ids(s)[0])

    def lat_out(b, s):
        direction, gid, _, lc = ids(s)
        return (b, jnp.where(jnp.logical_and(direction == 1, gid >= ncc), lc, nlc - 1), 0)

    def ctx_out(b, s):
        direction, gid, cc, _ = ids(s)
        return (b, jnp.where(direction == 1, jnp.where(gid < ncc, cc, 0), ncc - 1), 0)

    def by_dir(b, s):
        return (ids(s)[0], 0, 0)

    const2 = lambda b, s: (0, 0)
    const3 = lambda b, s: (0, 0, 0)
    seqs = lambda t, n: t.reshape(n_batch, n, t.shape[-1])
    blk = lambda width, index_map: pl.BlockSpec((SSD_SEQS, T, width), index_map)
    out_l, out_c = pl.pallas_call(
        functools.partial(_ssd_kernel, n_chunks=n_chunks, n_ctx_chunks=ncc),
        out_shape=(jax.ShapeDtypeStruct((n_batch, seq_len, SSM_D_INNER), BF16),
                   jax.ShapeDtypeStruct((n_batch, ctx_len, SSM_D_INNER), BF16)),
        grid=(n_batch // SSD_SEQS, 2 * n_chunks),
        in_specs=[
            blk(SSM_XBC, lat_main),
            blk(SSM_XBC, ctx_main),
            blk(LANES, lat_dt),
            blk(LANES, ctx_dt),
            blk(SSM_D_INNER, lat_out),
            blk(SSM_D_INNER, ctx_out),
            pl.BlockSpec((None, 1, LANES), by_dir),
            pl.BlockSpec((1, SSM_D_INNER), const2),
            pl.BlockSpec((1, SSM_D_INNER), const2),
            pl.BlockSpec((2, T, T), const3),
            pl.BlockSpec((LANES, SSM_D_INNER), const2),
        ],
        out_specs=(blk(SSM_D_INNER, lat_out), blk(SSM_D_INNER, ctx_out)),
        scratch_shapes=[
            pltpu.VMEM((SSD_SEQS, T, SSM_XBC), BF16),
            pltpu.VMEM((SSD_SEQS, T, LANES), F32),
            pltpu.VMEM((SSD_SEQS, T, SSM_D_INNER), BF16),
            pltpu.VMEM((SSD_SEQS, n_chunks, T, SSM_D_INNER), F32),
            pltpu.VMEM((SSD_SEQS, SSM_GROUPS, SSM_D_STATE, SSM_D_INNER // SSM_GROUPS), F32),
        ],
        compiler_params=_params(("arbitrary", "arbitrary")),
        name="ssd_mixer",
    )(seqs(xbc_l, seq_len), seqs(xbc_c, ctx_len), seqs(dt_l, seq_len), seqs(dt_c, ctx_len),
      seqs(z_l, seq_len), seqs(z_c, ctx_len), a_log_p, d_skip_x, norm_w, tri, expand)
    return out_l.reshape(n_batch * seq_len, SSM_D_INNER), out_c.reshape(n_batch * ctx_len, SSM_D_INNER)


def _merge_kernel(h_ref, na_ref, ssm_ref, gqa_ref, gate_ref, g_ref, w1_ref, w2_ref, w3_ref, wo_ref, o_ref):
    gates = gate_ref[...]
    t = _sigmoid(gates[:, :D_MODEL].astype(F32)) * _mm(na_ref[...], w1_ref[...])
    t = t + _sigmoid(gates[:, D_MODEL:2 * D_MODEL].astype(F32)) * _mm(ssm_ref[...], w2_ref[...])
    t = t + _sigmoid(gates[:, 2 * D_MODEL:].astype(F32)) * _mm(gqa_ref[...], w3_ref[...])
    o_ref[...] = h_ref[...] + g_ref[...] * _mm(t.astype(BF16), wo_ref[...])


def _merge(h, o_na, o_ssm, o_gqa, gates, mod4, row_fn, w1, w2, w3, wo, tm):
    n_tok = h.shape[0]
    tok = lambda width: pl.BlockSpec((tm, width), lambda i: (i, 0))
    return pl.pallas_call(
        _merge_kernel,
        out_shape=jax.ShapeDtypeStruct((n_tok, D_MODEL), F32),
        grid=(n_tok // tm,),
        in_specs=[tok(D_MODEL), tok(NA_W), tok(SSM_D_INNER), tok(GQA_Q), tok(3 * D_MODEL),
                  _mod_spec(2, row_fn),
                  _resident((NA_W, D_MODEL)), _resident((SSM_D_INNER, D_MODEL)),
                  _resident((GQA_Q, D_MODEL)), _resident((D_MODEL, D_MODEL))],
        out_specs=tok(D_MODEL),
        compiler_params=_params(("arbitrary",)),
        name="branch_merge",
    )(h, o_na, o_ssm, o_gqa, gates, mod4, w1, w2, w3, wo)


def _ffn_kernel(h_ref, hp_ref, hn_ref, sh_ref, sc_ref, g_ref, nw_ref, wup_ref, cw_ref, wdn_ref, fw_ref,
                o_ref, hn_sc, u_sc, acc_sc, *, tm, tiles_per_seq, final_norm):
    j = pl.program_id(0) % tiles_per_seq
    seq_first = j == 0
    seq_last = j == tiles_per_seq - 1

    def norm_mod(x):
        return (_rms(x) * nw_ref[...]) * (1.0 + sc_ref[...]) + sh_ref[...]

    x = h_ref[...]
    hn_sc[0:tm] = norm_mod(x).astype(BF16)
    before = jnp.where(seq_first, 0.0, norm_mod(hp_ref[...]))
    after = jnp.where(seq_last, 0.0, norm_mod(hn_ref[...]))
    hn_sc[tm:] = jnp.concatenate([after, before], axis=0).astype(BF16)
    acc_sc[...] = jnp.zeros_like(acc_sc)

    def up(c):
        u_sc[c % 2] = _mm(hn_sc[...], wup_ref[c])

    up(0)
    for c in range(FFN_CHUNKS):
        if c + 1 < FFN_CHUNKS:
            up(c + 1)
        y = _conv3_rows(u_sc[c % 2], tm, cw_ref[c])
        act = (_silu(y[:, :FFN_TN]) * y[:, FFN_TN:]).astype(BF16)
        acc_sc[...] += _mm(act, wdn_ref[c])
    out = x + g_ref[...] * acc_sc[...]
    if final_norm:
        out = _rms(out) * fw_ref[...]
    o_ref[...] = out


def _conv_ffn(h, mod4, row_fn, norm_w, w_up_c, conv_c, w_down_c, final_w, tm, seq_len, final_norm):
    n_tok = h.shape[0]
    return pl.pallas_call(
        functools.partial(_ffn_kernel, tm=tm, tiles_per_seq=seq_len // tm, final_norm=final_norm),
        out_shape=jax.ShapeDtypeStruct((n_tok, D_MODEL), F32),
        grid=(n_tok // tm,),
        in_specs=[
            pl.BlockSpec((tm, D_MODEL), lambda i: (i, 0)),
            *_halo_specs(tm, n_tok),
            _mod_spec(3, row_fn),
            _mod_spec(4, row_fn),
            _mod_spec(5, row_fn),
            _resident((1, D_MODEL)),
            _resident((FFN_CHUNKS, D_MODEL, 2 * FFN_TN)),
            _resident((FFN_CHUNKS, F32_ROWS, 2 * FFN_TN)),
            _resident((FFN_CHUNKS, FFN_TN, D_MODEL)),
            _resident((1, D_MODEL)),
        ],
        out_specs=pl.BlockSpec((tm, D_MODEL), lambda i: (i, 0)),
        scratch_shapes=[pltpu.VMEM((tm + 2 * F32_ROWS, D_MODEL), BF16),
                        pltpu.VMEM((2, tm + 2 * F32_ROWS, 2 * FFN_TN), F32),
                        pltpu.VMEM((tm, D_MODEL), F32)],
        compiler_params=_params(("arbitrary",)),
        name="conv_ffn",
    )(h, h, h, mod4, mod4, mod4, norm_w, w_up_c, conv_c, w_down_c, final_w)


def _rope_tables(seq_len):
    n_freq = HEAD_DIM // 4
    inv_freq = ROPE_THETA ** (-jnp.arange(n_freq, dtype=F32) / n_freq)
    t = jnp.arange(seq_len, dtype=jnp.int32)
    row = (t // GRID_W).astype(F32)
    col = (t % GRID_W).astype(F32)
    ang = jnp.concatenate([row[:, None] * inv_freq, col[:, None] * inv_freq], axis=-1)
    cos = jnp.repeat(jnp.cos(ang), 2, axis=-1)
    sin = jnp.repeat(jnp.sin(ang), 2, axis=-1) * jnp.tile(jnp.asarray([-1.0, 1.0], F32), HEAD_DIM // 2)
    return jnp.tile(cos, (1, LANES // HEAD_DIM)), jnp.tile(sin, (1, LANES // HEAD_DIM))


def _na_bias_mask(rel_bias):
    qc = np.arange(GRID_W)[:, None]
    kc = np.arange(GRID_W)[None, :]
    win0 = np.clip(qc - NA_WIN_COLS // 2, 0, GRID_W - NA_WIN_COLS)
    ok = (kc >= win0) & (kc < win0 + NA_WIN_COLS)
    dcol = np.clip(kc - qc + NA_WIN_COLS - 1, 0, 2 * NA_WIN_COLS - 2)
    per_drow = jnp.where(jnp.asarray(ok)[None, None], rel_bias.astype(F32)[:, :, dcol], NEG_BIG)
    n_heads = rel_bias.shape[0]
    masked = jnp.full((n_heads, GRID_W, GRID_W), NEG_BIG, F32)
    half = NA_WIN_ROWS // 2
    classes = []
    for cls in range(3):
        rows = []
        for j in range(NA_ROW_GROUP):
            first_slab_row = (0, j, NA_SLAB_ROWS - NA_WIN_ROWS)[cls]
            drow0 = (NA_WIN_ROWS - 1 - j, NA_WIN_ROWS - 1 - half, NA_WIN_ROWS - 1 - half - j)[cls]
            blocks = [per_drow[:, drow0 + sr - first_slab_row] if 0 <= sr - first_slab_row < NA_WIN_ROWS else masked
                      for sr in range(NA_SLAB_ROWS)]
            rows.append(jnp.concatenate(blocks, axis=-1))
        classes.append(jnp.concatenate(rows, axis=1))
    table = jnp.stack(classes, axis=1)
    table = table.reshape(n_heads // 2, 2, 3, NA_ROW_GROUP * GRID_W, NA_SLAB_ROWS * GRID_W)
    return jnp.moveaxis(table, 1, 2).reshape(n_heads // 2, 3, 2 * NA_ROW_GROUP * GRID_W, NA_SLAB_ROWS * GRID_W)


def _pack_w_in(w_in):
    sizes = [3 * NA_W, SSM_D_INNER, SSM_XBC, 2 * SSM_HEADS, GQA_Q, GQA_KV, GQA_KV]
    na, z, xbc, dt, gq, gk, gv, gate = jnp.split(w_in, [int(i) for i in np.cumsum(sizes)], axis=-1)
    na = jnp.concatenate([na[:, :NA_W] * HEAD_DIM ** -0.5, na[:, NA_W:]], axis=-1)
    w_main = jnp.concatenate([na, z, xbc, gq, gk, gv, gate], axis=-1).astype(BF16)
    dt_pad = jnp.zeros((D_MODEL, DT_COLS), F32)
    dt_pad = dt_pad.at[:, :SSM_HEADS].set(dt[:, :SSM_HEADS]).at[:, LANES:LANES + SSM_HEADS].set(dt[:, SSM_HEADS:])
    dt_hi, dt_lo = _split_bf16(dt_pad)
    return w_main, dt_hi, dt_lo


def _pad_heads(v):
    return jnp.zeros((2, 1, LANES), F32).at[:, 0, :SSM_HEADS].set(v.astype(F32))


def _pack_ffn(w_up, conv_w, conv_b, w_down):
    def chunked(t):
        a, b = t[..., :FFN_HIDDEN], t[..., FFN_HIDDEN:]
        lead = t.shape[:-1]
        return jnp.concatenate([a.reshape(lead + (FFN_CHUNKS, FFN_TN)), b.reshape(lead + (FFN_CHUNKS, FFN_TN))], axis=-1)

    w_up_c = jnp.moveaxis(chunked(w_up), 1, 0).astype(BF16)
    taps = jnp.concatenate([conv_w, conv_b[None], jnp.zeros((F32_ROWS - 4, 2 * FFN_HIDDEN), F32)], axis=0)
    conv_c = jnp.moveaxis(chunked(taps), 1, 0)
    w_down_c = w_down.reshape(FFN_CHUNKS, FFN_TN, D_MODEL).astype(BF16)
    return w_up_c, conv_c, w_down_c


def _scan_masks():
    t = np.arange(SSM_CHUNK)
    tril = (t[:, None] >= t[None, :]).astype(np.float32)
    return jnp.asarray(np.stack([tril, tril.T]), BF16)


def kernel(x, c, ctx, c_ctx, w_mod, b_mod, norm1_w, norm2_w, w_in, na_rel_bias, ssm_conv_w, ssm_conv_b, ssm_a_log, ssm_dt_bias, ssm_d, ssm_norm_w, q_norm_w, k_norm_w, w_out_na, w_out_ssm, w_out_gqa, w_o, ffn_w_up, ffn_conv_w, ffn_conv_b, ffn_w_down, final_norm_w):
    n_batch, seq_len, _ = x.shape
    ctx_len = ctx.shape[1]
    depth = w_mod.shape[0]
    assert seq_len % (GRID_W * NA_ROW_GROUP) == 0 and seq_len // GRID_W >= NA_SLAB_ROWS
    assert n_batch <= CTX_MOD_ROW and n_batch % SSD_SEQS == 0 and seq_len % 512 == 0 and ctx_len % 256 == 0
    tm_lat, tm_ctx = 512, 256

    cc = jnp.zeros((MOD_ROWS, D_MODEL), F32).at[:n_batch].set(c).at[CTX_MOD_ROW].set(c_ctx)
    mods = _mod_vectors(cc, w_mod, b_mod)
    lat_row = lambda i: i // (seq_len // tm_lat)
    ctx_row = lambda i: CTX_MOD_ROW

    cos_l, sin_l = _rope_tables(seq_len)
    cos_c, sin_c = jnp.ones((ctx_len, LANES), F32), jnp.zeros((ctx_len, LANES), F32)
    head_of_lane = np.arange(LANES) // HEAD_DIM
    same_head = head_of_lane[:, None] == head_of_lane[None, :]
    seg_ones = jnp.asarray(np.concatenate([same_head, same_head], axis=0), BF16)
    tri = _scan_masks()
    lane_head = np.arange(SSM_D_INNER) // HEAD_DIM
    expand = jnp.asarray(np.arange(LANES)[:, None] == lane_head[None, :], BF16)

    h_lat = x.reshape(n_batch * seq_len, D_MODEL)
    h_ctx = ctx.reshape(n_batch * ctx_len, D_MODEL)
    row = lambda v: v.reshape(1, -1).astype(F32)
    for layer in range(depth):
        need_ctx = layer < depth - 1
        last = layer == depth - 1
        mod4 = mods[layer].reshape(MOD_ROWS, N_MOD, 1, D_MODEL)
        w_main, w_dt_hi, w_dt_lo = _pack_w_in(w_in[layer])
        qw = row(jnp.tile(q_norm_w[layer], LANES // HEAD_DIM))
        kw = row(jnp.tile(k_norm_w[layer], LANES // HEAD_DIM))
        conv_taps = jnp.concatenate(
            [ssm_conv_w[layer], ssm_conv_b[layer][None], jnp.zeros((F32_ROWS - 4, SSM_XBC), F32)], axis=0)
        dt_bias = jnp.zeros((1, DT_COLS), F32)
        dt_bias = dt_bias.at[0, :SSM_HEADS].set(ssm_dt_bias[layer, 0]).at[0, LANES:LANES + SSM_HEADS].set(
            ssm_dt_bias[layer, 1])
        proj = functools.partial(_in_proj, mod4=mod4, norm_w=row(norm1_w[layer]), qw=qw, kw=kw, seg_ones=seg_ones,
                                 conv_taps=conv_taps, dt_bias=dt_bias,
                                 w_main=w_main, w_dt_hi=w_dt_hi, w_dt_lo=w_dt_lo)
        na_l, z_l, xbc_l, dt_l, q_l, k_l, v_l, gate_l = proj(
            h_lat, row_fn=lat_row, cos=cos_l, sin=sin_l, tm=tm_lat, seq_len=seq_len)
        na_c, z_c, xbc_c, dt_c, q_c, k_c, v_c, gate_c = proj(
            h_ctx, row_fn=ctx_row, cos=cos_c, sin=sin_c, tm=tm_ctx, seq_len=ctx_len)

        o_na_l = _neighbourhood_attention(na_l, na_c, _na_bias_mask(na_rel_bias[layer]), n_batch, seq_len, ctx_len)
        o_ssm_l, o_ssm_c = _ssd_mixer(
            xbc_l, dt_l, z_l, xbc_c, dt_c, z_c, _pad_heads(ssm_a_log[layer]),
            row(jnp.repeat(ssm_d[layer], HEAD_DIM)), row(ssm_norm_w[layer]), tri, expand,
            n_batch, seq_len, ctx_len)
        per_batch = lambda t, n: t.reshape(GQA_KV_HEADS, n_batch, n, HEAD_DIM)
        with_ones = lambda v: jnp.concatenate(
            [v, jnp.ones(v.shape[:-1] + (1,), BF16), jnp.zeros(v.shape[:-1] + (LANES - HEAD_DIM - 1,), BF16)], axis=-1)
        k_ctx, v_ctx = per_batch(k_c, ctx_len), with_ones(per_batch(v_c, ctx_len))
        k_all = jnp.concatenate([k_ctx, per_batch(k_l, seq_len)], axis=2)
        v_all = jnp.concatenate([v_ctx, with_ones(per_batch(v_l, seq_len))], axis=2)
        o_gqa_l = _gqa_attention(q_l, k_all, v_all, n_batch, seq_len, tq=256)

        w1, w2, w3, wo = (w.astype(BF16) for w in (w_out_na[layer], w_out_ssm[layer], w_out_gqa[layer], w_o[layer]))
        ffn_w = _pack_ffn(ffn_w_up[layer], ffn_conv_w[layer], ffn_conv_b[layer], ffn_w_down[layer])
        h_lat = _merge(h_lat, o_na_l, o_ssm_l, o_gqa_l, gate_l, mod4, lat_row, w1, w2, w3, wo, tm_lat)
        h_lat = _conv_ffn(h_lat, mod4, lat_row, row(norm2_w[layer]), *ffn_w, row(final_norm_w),
                          tm=tm_lat, seq_len=seq_len, final_norm=last)
        if need_ctx:
            o_na_c = _context_mha(na_c, n_batch, ctx_len)
            o_gqa_c = _gqa_attention(q_c, k_ctx, v_ctx, n_batch, ctx_len, tq=ctx_len)
            h_ctx = _merge(h_ctx, o_na_c, o_ssm_c, o_gqa_c, gate_c, mod4, ctx_row, w1, w2, w3, wo, tm_ctx)
            h_ctx = _conv_ffn(h_ctx, mod4, ctx_row, row(norm2_w[layer]), *ffn_w, row(final_norm_w),
                              tm=tm_ctx, seq_len=ctx_len, final_norm=False)
    return h_lat.reshape(n_batch, seq_len, D_MODEL)
```

```python
import functools

import numpy as np
import jax
import jax.numpy as jnp
from jax import lax
from jax.experimental import pallas as pl
from jax.experimental.pallas import tpu as pltpu

F32 = jnp.float32
BF16 = jnp.bfloat16

D_MODEL = 1024
GRID_W = 64
EPS = 1e-6
HEAD_DIM = 64
NA_HEADS = 8
NA_W = NA_HEADS * HEAD_DIM
NA_WIN_ROWS = 8
NA_WIN_COLS = 16
NA_ROW_GROUP = 4
NA_SLAB_ROWS = 12
SSM_HEADS = 16
SSM_D_INNER = SSM_HEADS * HEAD_DIM
SSM_GROUPS = 2
SSM_D_STATE = 64
SSM_BC = SSM_GROUPS * SSM_D_STATE
SSM_XBC = SSM_D_INNER + 2 * SSM_BC
SSM_CHUNK = 128
SSD_SEQS = 2
GQA_HEADS = 8
GQA_KV_HEADS = 2
GQA_GROUP = GQA_HEADS // GQA_KV_HEADS
GQA_Q = GQA_HEADS * HEAD_DIM
GQA_KV = GQA_KV_HEADS * HEAD_DIM
GQA_BK = 256
GQA_ROW_CHUNK = 128
ROPE_THETA = 10000.0
FFN_HIDDEN = 2816
N_MOD = 6
MOD_ROWS = 16
CTX_MOD_ROW = 8
NEG_BIG = -1e30

LANES = 128
BF16_ROWS = 16
F32_ROWS = 8
VMEM_LIMIT = 56 * 1024 * 1024

SEG_NA = (0, 3 * NA_W)
SEG_Z = (SEG_NA[1], SEG_NA[1] + SSM_D_INNER)
SEG_XBC = (SEG_Z[1], SEG_Z[1] + SSM_XBC)
SEG_GQ = (SEG_XBC[1], SEG_XBC[1] + GQA_Q)
SEG_GK = (SEG_GQ[1], SEG_GQ[1] + GQA_KV)
SEG_GV = (SEG_GK[1], SEG_GK[1] + GQA_KV)
SEG_GATE = (SEG_GV[1], SEG_GV[1] + 3 * D_MODEL)
W_MAIN_COLS = SEG_GATE[1]
DT_COLS = 2 * LANES
DOT_COLS = 512

FFN_TN = 256
FFN_CHUNKS = FFN_HIDDEN // FFN_TN


def _mm(a, b):
    return jnp.dot(a, b, preferred_element_type=F32)


def _mm_nt(a, b):
    return lax.dot_general(a, b, (((1,), (1,)), ((), ())), preferred_element_type=F32)


def _split_bf16(x):
    hi = x.astype(BF16)
    lo = (x - hi.astype(F32)).astype(BF16)
    return hi, lo


def _sigmoid(x):
    return 1.0 / (1.0 + jnp.exp(-x))


def _silu(x):
    return x * _sigmoid(x)


def _softplus(x):
    return jnp.maximum(x, 0.0) + jnp.log1p(jnp.exp(-jnp.abs(x)))


def _rms(x):
    return x * lax.rsqrt(jnp.mean(x * x, axis=-1, keepdims=True) + EPS)


def _resident(shape):
    nd = len(shape)
    return pl.BlockSpec(shape, lambda *_: (0,) * nd, pipeline_mode=pl.Buffered(1))


def _params(sem):
    return pltpu.CompilerParams(dimension_semantics=sem, vmem_limit_bytes=VMEM_LIMIT)


def _mod_kernel(c_ref, w_ref, b_ref, o_ref):
    x_hi, x_lo = _split_bf16(_silu(c_ref[...]))
    w_hi, w_lo = _split_bf16(w_ref[...])
    o_ref[...] = _mm(x_hi, w_hi) + _mm(x_lo, w_hi) + _mm(x_hi, w_lo) + b_ref[...]


def _mod_vectors(cc, w_mod, b_mod):
    n_layers = w_mod.shape[0]
    tn = D_MODEL
    return pl.pallas_call(
        _mod_kernel,
        out_shape=jax.ShapeDtypeStruct((n_layers, MOD_ROWS, N_MOD * D_MODEL), F32),
        grid=(n_layers, N_MOD),
        in_specs=[
            pl.BlockSpec((MOD_ROWS, D_MODEL), lambda l, j: (0, 0)),
            pl.BlockSpec((None, D_MODEL, tn), lambda l, j: (l, 0, j)),
            pl.BlockSpec((None, 1, tn), lambda l, j: (l, 0, j)),
        ],
        out_specs=pl.BlockSpec((None, MOD_ROWS, tn), lambda l, j: (l, 0, j)),
        compiler_params=_params(("arbitrary", "arbitrary")),
        name="mod_vectors",
    )(cc, w_mod, b_mod.reshape(n_layers, 1, N_MOD * D_MODEL))


def _mod_spec(which, row_fn):
    return pl.BlockSpec((None, None, 1, D_MODEL), lambda i: (row_fn(i), which, 0, 0))


def _qk_norm_rope(x, w, cos, sin, seg_ones):
    ss = _mm(jnp.concatenate(_split_bf16(x * x), axis=1), seg_ones)
    xn = x * lax.rsqrt(ss * (1.0 / HEAD_DIM) + EPS) * w
    lane = lax.broadcasted_iota(jnp.int32, xn.shape, 1)
    partner = jnp.where((lane & 1) == 0, pltpu.roll(xn, LANES - 1, 1), pltpu.roll(xn, 1, 1))
    return xn * cos + partner * sin


def _conv3_rows(u, tm, taps):
    n = u.shape[0]
    return (taps[0:1] * pltpu.roll(u, 1, 0)[0:tm] + taps[1:2] * u[0:tm]
            + taps[2:3] * pltpu.roll(u, n - 1, 0)[0:tm] + taps[3:4])


def _inproj_kernel(h_ref, hp_ref, hn_ref, sh_ref, sc_ref, nw_ref, cos_ref, sin_ref, qw_ref, kw_ref, seg_ref,
                   cw_ref, dtb_ref, w_ref, wdh_ref, wdl_ref,
                   na_ref, z_ref, xbc_ref, dt_ref, q_ref, k_ref, v_ref, gate_ref, y_sc, *, tm, tiles_per_seq):
    j = pl.program_id(0) % tiles_per_seq

    def norm_mod(x):
        return (_rms(x) * nw_ref[...]) * (1.0 + sc_ref[...]) + sh_ref[...]

    y = norm_mod(h_ref[...])
    y_hi, y_lo = _split_bf16(y)
    before = jnp.where(j == 0, 0.0, norm_mod(hp_ref[...]))
    after = jnp.where(j == tiles_per_seq - 1, 0.0, norm_mod(hn_ref[...]))
    y_sc[0:tm] = y_hi
    y_sc[tm:] = jnp.concatenate([after, before], axis=0).astype(BF16)

    def project(out_ref, seg):
        for c0 in range(seg[0], seg[1], DOT_COLS):
            c1 = min(c0 + DOT_COLS, seg[1])
            out_ref[:, c0 - seg[0]:c1 - seg[0]] = _mm(y_hi, w_ref[:, c0:c1]).astype(out_ref.dtype)

    for c0 in range(SEG_XBC[0], SEG_XBC[1], DOT_COLS):
        c1 = min(c0 + DOT_COLS, SEG_XBC[1])
        cols = slice(c0 - SEG_XBC[0], c1 - SEG_XBC[0])
        xbc_ref[:, cols] = _silu(_conv3_rows(_mm(y_sc[...], w_ref[:, c0:c1]), tm, cw_ref[:, cols])).astype(BF16)
    dt_raw = _mm(y_hi, wdh_ref[...]) + _mm(y_lo, wdh_ref[...]) + _mm(y_hi, wdl_ref[...])
    dt_ref[...] = _softplus(dt_raw + dtb_ref[...])

    cos = cos_ref[...]
    sin = sin_ref[...]
    seg_ones = seg_ref[...]
    for pair in range(GQA_Q // LANES):
        c0 = SEG_GQ[0] + pair * LANES
        xr = _qk_norm_rope(_mm(y_hi, w_ref[:, c0:c0 + LANES]), qw_ref[...], cos, sin, seg_ones)
        xr = (xr * HEAD_DIM ** -0.5).astype(BF16)
        q_ref[2 * pair] = xr[:, :HEAD_DIM]
        q_ref[2 * pair + 1] = xr[:, HEAD_DIM:]
    kr = _qk_norm_rope(_mm(y_hi, w_ref[:, SEG_GK[0]:SEG_GK[1]]), kw_ref[...], cos, sin, seg_ones).astype(BF16)
    k_ref[0] = kr[:, :HEAD_DIM]
    k_ref[1] = kr[:, HEAD_DIM:]
    vv = _mm(y_hi, w_ref[:, SEG_GV[0]:SEG_GV[1]]).astype(BF16)
    v_ref[0] = vv[:, :HEAD_DIM]
    v_ref[1] = vv[:, HEAD_DIM:]
    project(na_ref, SEG_NA)
    project(z_ref, SEG_Z)
    project(gate_ref, SEG_GATE)


def _halo_specs(tm, n_tok):
    per_tile = tm // F32_ROWS
    last = n_tok // F32_ROWS - 1
    return [pl.BlockSpec((F32_ROWS, D_MODEL), lambda i: (jnp.maximum(i * per_tile - 1, 0), 0)),
            pl.BlockSpec((F32_ROWS, D_MODEL), lambda i: (jnp.minimum((i + 1) * per_tile, last), 0))]


def _in_proj(h, mod4, row_fn, norm_w, cos, sin, qw, kw, seg_ones, conv_taps, dt_bias, w_main, w_dt_hi, w_dt_lo,
             tm, seq_len):
    n_tok = h.shape[0]
    pos_tiles = seq_len // tm
    tok = lambda width: pl.BlockSpec((tm, width), lambda i: (i, 0))
    hm = lambda nh: pl.BlockSpec((nh, tm, HEAD_DIM), lambda i: (0, i, 0))
    outs = pl.pallas_call(
        functools.partial(_inproj_kernel, tm=tm, tiles_per_seq=pos_tiles),
        out_shape=(
            jax.ShapeDtypeStruct((n_tok, 3 * NA_W), BF16),
            jax.ShapeDtypeStruct((n_tok, SSM_D_INNER), BF16),
            jax.ShapeDtypeStruct((n_tok, SSM_XBC), BF16),
            jax.ShapeDtypeStruct((n_tok, DT_COLS), F32),
            jax.ShapeDtypeStruct((GQA_HEADS, n_tok, HEAD_DIM), BF16),
            jax.ShapeDtypeStruct((GQA_KV_HEADS, n_tok, HEAD_DIM), BF16),
            jax.ShapeDtypeStruct((GQA_KV_HEADS, n_tok, HEAD_DIM), BF16),
            jax.ShapeDtypeStruct((n_tok, 3 * D_MODEL), BF16),
        ),
        grid=(n_tok // tm,),
        in_specs=[
            tok(D_MODEL),
            *_halo_specs(tm, n_tok),
            _mod_spec(0, row_fn),
            _mod_spec(1, row_fn),
            _resident((1, D_MODEL)),
            pl.BlockSpec((tm, LANES), lambda i: (i % pos_tiles, 0)),
            pl.BlockSpec((tm, LANES), lambda i: (i % pos_tiles, 0)),
            _resident((1, LANES)),
            _resident((1, LANES)),
            _resident((2 * LANES, LANES)),
            _resident((F32_ROWS, SSM_XBC)),
            _resident((1, DT_COLS)),
            _resident((D_MODEL, W_MAIN_COLS)),
            _resident((D_MODEL, DT_COLS)),
            _resident((D_MODEL, DT_COLS)),
        ],
        out_specs=(tok(3 * NA_W), tok(SSM_D_INNER), tok(SSM_XBC), tok(DT_COLS),
                   hm(GQA_HEADS), hm(GQA_KV_HEADS), hm(GQA_KV_HEADS), tok(3 * D_MODEL)),
        scratch_shapes=[pltpu.VMEM((tm + 2 * F32_ROWS, D_MODEL), BF16)],
        compiler_params=_params(("arbitrary",)),
        name="in_proj",
    )(h, h, h, mod4, mod4, norm_w, cos, sin, qw, kw, seg_ones, conv_taps, dt_bias, w_main, w_dt_hi, w_dt_lo)
    return outs


def _softmax_pv(s_list, v_list):
    m = s_list[0].max(axis=-1, keepdims=True)
    for s in s_list[1:]:
        m = jnp.maximum(m, s.max(axis=-1, keepdims=True))
    den = 0.0
    out = 0.0
    for s, v in zip(s_list, v_list):
        p = jnp.exp(s - m)
        den = den + p.sum(axis=-1, keepdims=True)
        out = out + _mm(p.astype(BF16), v)
    return out / den


def _na_kernel(q_ref, k_ref, v_ref, kc_ref, vc_ref, bm_ref, o_ref, *, n_rows):
    n_groups = n_rows // NA_ROW_GROUP
    gq = NA_ROW_GROUP * GRID_W
    slab = NA_SLAB_ROWS * GRID_W
    kc = kc_ref[...]
    vc = vc_ref[...]
    head0 = lax.broadcasted_iota(jnp.int32, (1, LANES), 1) < HEAD_DIM

    def group(i, carry):
        r0 = i * NA_ROW_GROUP
        start = jnp.clip(r0 - NA_WIN_ROWS // 2, 0, n_rows - NA_SLAB_ROWS)
        cls = jnp.where(i == 0, 0, jnp.where(i == n_groups - 1, 2, 1))
        q2 = q_ref[pl.ds(pl.multiple_of(r0 * GRID_W, gq), gq), :]
        ks = k_ref[pl.ds(pl.multiple_of(start * GRID_W, GRID_W), slab), :]
        vs = v_ref[pl.ds(pl.multiple_of(start * GRID_W, GRID_W), slab), :]
        zero = jnp.zeros_like(q2)
        qs = jnp.concatenate([jnp.where(head0, q2, zero), jnp.where(head0, zero, q2)], axis=0)
        s_win = _mm_nt(qs, ks) + bm_ref[cls]
        s_ctx = _mm_nt(qs, kc)
        o = _softmax_pv([s_win, s_ctx], [vs, vc])
        o_ref[pl.ds(pl.multiple_of(r0 * GRID_W, gq), gq), :] = jnp.where(head0, o[:gq], o[gq:]).astype(BF16)
        return carry

    lax.fori_loop(0, n_groups, group, 0, unroll=8)


def _neighbourhood_attention(na_l, na_c, bias_mask, n_batch, seq_len, ctx_len):
    n_rows = seq_len // GRID_W
    n_pairs = NA_W // LANES
    lat = lambda part: pl.BlockSpec((seq_len, LANES), lambda b, p: (b, part * n_pairs + p))
    ctx = lambda part: pl.BlockSpec((ctx_len, LANES), lambda b, p: (b, part * n_pairs + p))
    return pl.pallas_call(
        functools.partial(_na_kernel, n_rows=n_rows),
        out_shape=jax.ShapeDtypeStruct((n_batch * seq_len, NA_W), BF16),
        grid=(n_batch, n_pairs),
        in_specs=[lat(0), lat(1), lat(2), ctx(1), ctx(2),
                  pl.BlockSpec((None, 3, 2 * NA_ROW_GROUP * GRID_W, NA_SLAB_ROWS * GRID_W),
                               lambda b, p: (p, 0, 0, 0))],
        out_specs=pl.BlockSpec((seq_len, LANES), lambda b, p: (b, p)),
        compiler_params=_params(("arbitrary", "arbitrary")),
        name="neighbourhood_attention",
    )(na_l, na_l, na_l, na_c, na_c, bias_mask)


def _ctx_mha_kernel(q_ref, k_ref, v_ref, o_ref):
    q2 = q_ref[...]
    k2 = k_ref[...]
    v2 = v_ref[...]
    head0 = lax.broadcasted_iota(jnp.int32, (1, LANES), 1) < HEAD_DIM
    outs = []
    for hh in range(2):
        keep = head0 if hh == 0 else jnp.logical_not(head0)
        qm = jnp.where(keep, q2, jnp.zeros_like(q2))
        outs.append(_softmax_pv([_mm_nt(qm, k2)], [v2]))
    o_ref[...] = jnp.where(head0, outs[0], outs[1]).astype(BF16)


def _context_mha(na_c, n_batch, ctx_len):
    n_pairs = NA_W // LANES
    part = lambda which: pl.BlockSpec((ctx_len, LANES), lambda b, p: (b, which * n_pairs + p))
    return pl.pallas_call(
        _ctx_mha_kernel,
        out_shape=jax.ShapeDtypeStruct((n_batch * ctx_len, NA_W), BF16),
        grid=(n_batch, n_pairs),
        in_specs=[part(0), part(1), part(2)],
        out_specs=pl.BlockSpec((ctx_len, LANES), lambda b, p: (b, p)),
        compiler_params=_params(("arbitrary", "arbitrary")),
        name="context_mha",
    )(na_c, na_c, na_c)


def _gqa_kernel(q_ref, k_ref, v_ref, o_ref, s_sc, p_sc, m_sc, a_sc, acc_sc, *, tq, blocks):
    rows = GQA_GROUP * tq
    q = q_ref[...].reshape(rows, HEAD_DIM)
    m_sc[...] = jnp.full_like(m_sc, NEG_BIG)
    acc_sc[...] = jnp.zeros_like(acc_sc)

    def scores(blk, slot):
        k0, size = blk
        s_sc[slot, :, 0:size] = _mm_nt(q, k_ref[k0:k0 + size, :])

    def absorb(blk, slot):
        k0, size = blk
        for c0 in range(0, rows, GQA_ROW_CHUNK):
            rs = slice(c0, c0 + GQA_ROW_CHUNK)
            tiles = [s_sc[slot, rs, t * LANES:(t + 1) * LANES] for t in range(size // LANES)]
            m_old = m_sc[rs, :]
            m_new = jnp.maximum(m_old, functools.reduce(jnp.maximum, tiles).max(axis=-1, keepdims=True))
            a_sc[rs, :] = jnp.exp(m_old - m_new)
            m_sc[rs, :] = m_new
            for t, s in enumerate(tiles):
                p_sc[rs, t * LANES:(t + 1) * LANES] = jnp.exp(s - m_new).astype(BF16)
        acc_sc[...] = a_sc[...] * acc_sc[...] + _mm(p_sc[:, 0:size], v_ref[k0:k0 + size, :])

    scores(blocks[0], 0)
    for j, blk in enumerate(blocks):
        if j + 1 < len(blocks):
            scores(blocks[j + 1], (j + 1) % 2)
        absorb(blk, j % 2)
    acc = acc_sc[...]
    o = acc[:, :HEAD_DIM] / acc[:, HEAD_DIM:HEAD_DIM + 1]
    o_ref[...] = jnp.concatenate([o[g * tq:(g + 1) * tq] for g in range(GQA_GROUP)], axis=-1).astype(BF16)


def _gqa_attention(q_hm, k_all, v_aug, n_batch, q_len, tq):
    q_tiles = q_len // tq
    nk = k_all.shape[2]
    assert nk % (2 * LANES) == 0
    first = nk % GQA_BK
    blocks = ([(0, first)] if first else []) + [(k0, GQA_BK) for k0 in range(first, nk, GQA_BK)]
    rows = GQA_GROUP * tq
    return pl.pallas_call(
        functools.partial(_gqa_kernel, tq=tq, blocks=tuple(blocks)),
        out_shape=jax.ShapeDtypeStruct((n_batch * q_len, GQA_Q), BF16),
        grid=(n_batch, GQA_KV_HEADS, q_tiles),
        in_specs=[
            pl.BlockSpec((GQA_GROUP, tq, HEAD_DIM), lambda b, g, i: (g, b * q_tiles + i, 0)),
            pl.BlockSpec((None, None, nk, HEAD_DIM), lambda b, g, i: (g, b, 0, 0)),
            pl.BlockSpec((None, None, nk, LANES), lambda b, g, i: (g, b, 0, 0)),
        ],
        out_specs=pl.BlockSpec((tq, GQA_GROUP * HEAD_DIM), lambda b, g, i: (b * q_tiles + i, g)),
        scratch_shapes=[
            pltpu.VMEM((2, rows, GQA_BK), F32),
            pltpu.VMEM((rows, GQA_BK), BF16),
            pltpu.VMEM((rows, LANES), F32),
            pltpu.VMEM((rows, LANES), F32),
            pltpu.VMEM((rows, LANES), F32),
        ],
        compiler_params=_params(("arbitrary", "arbitrary", "arbitrary")),
        name="gqa_attention",
    )(q_hm, k_all, v_aug)


def _ssd_kernel(xl_ref, xc_ref, dtl_ref, dtc_ref, zl_ref, zc_ref, alog_ref, dsk_ref, nw_ref, tri_ref, exp_ref,
                outl_ref, outc_ref, x_sc, dt_sc, z_sc, y_sc, st_sc, *, n_chunks, n_ctx_chunks):
    s = pl.program_id(1)
    direction = s // n_chunks
    sp = s % n_chunks
    gid = jnp.where(direction == 0, sp,
                    jnp.where(sp < n_ctx_chunks, n_ctx_chunks - 1 - sp, n_chunks + n_ctx_chunks - 1 - sp))
    is_ctx = gid < n_ctx_chunks

    def stage(x_ref, dt_ref, z_ref):
        x_sc[...] = x_ref[...]
        dt_sc[...] = dt_ref[...]
        z_sc[...] = z_ref[...]

    @pl.when(is_ctx)
    def _():
        stage(xc_ref, dtc_ref, zc_ref)

    @pl.when(jnp.logical_not(is_ctx))
    def _():
        stage(xl_ref, dtl_ref, zl_ref)

    @pl.when(sp == 0)
    def _():
        st_sc[...] = jnp.zeros_like(st_sc)

    tri = tri_ref[direction]
    causal = tri > 0
    neg_a = -jnp.exp(alog_ref[...])
    expand = exp_ref[...]
    heads_per_group = SSM_HEADS // SSM_GROUPS
    group_lanes = heads_per_group * HEAD_DIM

    def chunk_scan(e):
        xs = x_sc[e, :, :SSM_D_INNER]
        dt = dt_sc[e]
        da = dt * neg_a
        da_hi, da_lo = _split_bf16(da)
        cs = _mm(tri, da_hi) + _mm(tri, da_lo)
        tot = da.sum(axis=0, keepdims=True)
        w = dt * jnp.exp(tot - cs)
        e_cs = jnp.exp(cs)
        cs_t = cs.T
        dt_t = dt.T
        w_hi, w_lo = _split_bf16(w)
        w_x = _mm(w_hi, expand) + _mm(w_lo, expand)
        e_hi, e_lo = _split_bf16(e_cs)
        e_cs_x = _mm(e_hi, expand) + _mm(e_lo, expand)
        t_hi, t_lo = _split_bf16(jnp.broadcast_to(jnp.exp(tot), (F32_ROWS, LANES)))
        e_tot_x = (_mm(t_hi, expand) + _mm(t_lo, expand))[0:1]
        xw = (xs.astype(F32) * w_x).astype(BF16)
        y_groups = []
        for g in range(SSM_GROUPS):
            b_g = x_sc[e, :, SSM_D_INNER + g * SSM_D_STATE:SSM_D_INNER + (g + 1) * SSM_D_STATE]
            c_g = x_sc[e, :, SSM_D_INNER + SSM_BC + g * SSM_D_STATE:SSM_D_INNER + SSM_BC + (g + 1) * SSM_D_STATE]
            cb = _mm_nt(c_g, b_g)
            lanes = slice(g * group_lanes, (g + 1) * group_lanes)
            st = st_sc[e, g]
            y_off = _mm(c_g, st.astype(BF16)) * e_cs_x[:, lanes]
            st_sc[e, g] = e_tot_x[:, lanes] * st + _mm(b_g.astype(F32).T.astype(BF16), xw[:, lanes])
            ys = []
            for h in range(g * heads_per_group, (g + 1) * heads_per_group):
                decay = jnp.exp(jnp.where(causal, cs[:, h:h + 1] - cs_t[h:h + 1, :], NEG_BIG))
                ys.append(_mm((cb * decay * dt_t[h:h + 1, :]).astype(BF16), xs[:, h * HEAD_DIM:(h + 1) * HEAD_DIM]))
            y_groups.append(jnp.concatenate(ys, axis=-1) + y_off)
        return jnp.concatenate(y_groups, axis=-1)

    ys = [chunk_scan(e) for e in range(SSD_SEQS)]

    @pl.when(direction == 0)
    def _():
        for e in range(SSD_SEQS):
            y_sc[e, gid] = ys[e] + dsk_ref[...] * x_sc[e, :, :SSM_D_INNER].astype(F32)

    @pl.when(direction == 1)
    def _():
        res = []
        for e in range(SSD_SEQS):
            gated = (y_sc[e, gid] + ys[e]) * _silu(z_sc[e].astype(F32))
            res.append((_rms(gated) * nw_ref[...]).astype(BF16))

        @pl.when(is_ctx)
        def _():
            for e in range(SSD_SEQS):
                outc_ref[e] = res[e]

        @pl.when(jnp.logical_not(is_ctx))
        def _():
            for e in range(SSD_SEQS):
                outl_ref[e] = res[e]


def _ssd_mixer(xbc_l, dt_l, z_l, xbc_c, dt_c, z_c, a_log_p, d_skip_x, norm_w, tri, expand,
               n_batch, seq_len, ctx_len):
    T = SSM_CHUNK
    nlc = seq_len // T
    ncc = ctx_len // T
    n_chunks = nlc + ncc

    def ids(s):
        direction = s // n_chunks
        sp = s % n_chunks
        gid = jnp.where(direction == 0, sp, jnp.where(sp < ncc, ncc - 1 - sp, n_chunks + ncc - 1 - sp))
        return direction, gid, jnp.clip(gid, 0, ncc - 1), jnp.clip(gid - ncc, 0, nlc - 1)

    def lat_main(b, s):
        return (b, ids(s)[3], 0)

    def ctx_main(b, s):
        return (b, ids(s)[2], 0)

    def lat_dt(b, s):
        return (b, ids(s)[3], ids(s)[0])

    def ctx_dt(b, s):
        return (b, ids(s)[2], ids(s)[0])

    def lat_out(b, s):
        direction, gid, _, lc = ids(s)
        return (b, jnp.where(jnp.logical_and(direction == 1, gid >= ncc), lc, nlc - 1), 0)

    def ctx_out(b, s):
        direction, gid, cc, _ = ids(s)
        return (b, jnp.where(direction == 1, jnp.where(gid < ncc, cc, 0), ncc - 1), 0)

    def by_dir(b, s):
        return (ids(s)[0], 0, 0)

    const2 = lambda b, s: (0, 0)
    const3 = lambda b, s: (0, 0, 0)
    seqs = lambda t, n: t.reshape(n_batch, n, t.shape[-1])
    blk = lambda width, index_map: pl.BlockSpec((SSD_SEQS, T, width), index_map)
    out_l, out_c = pl.pallas_call(
        functools.partial(_ssd_kernel, n_chunks=n_chunks, n_ctx_chunks=ncc),
        out_shape=(jax.ShapeDtypeStruct((n_batch, seq_len, SSM_D_INNER), BF16),
                   jax.ShapeDtypeStruct((n_batch, ctx_len, SSM_D_INNER), BF16)),
        grid=(n_batch // SSD_SEQS, 2 * n_chunks),
        in_specs=[
            blk(SSM_XBC, lat_main),
            blk(SSM_XBC, ctx_main),
            blk(LANES, lat_dt),
            blk(LANES, ctx_dt),
            blk(SSM_D_INNER, lat_out),
            blk(SSM_D_INNER, ctx_out),
            pl.BlockSpec((None, 1, LANES), by_dir),
            pl.BlockSpec((1, SSM_D_INNER), const2),
            pl.BlockSpec((1, SSM_D_INNER), const2),
            pl.BlockSpec((2, T, T), const3),
            pl.BlockSpec((LANES, SSM_D_INNER), const2),
        ],
        out_specs=(blk(SSM_D_INNER, lat_out), blk(SSM_D_INNER, ctx_out)),
        scratch_shapes=[
            pltpu.VMEM((SSD_SEQS, T, SSM_XBC), BF16),
            pltpu.VMEM((SSD_SEQS, T, LANES), F32),
            pltpu.VMEM((SSD_SEQS, T, SSM_D_INNER), BF16),
            pltpu.VMEM((SSD_SEQS, n_chunks, T, SSM_D_INNER), F32),
            pltpu.VMEM((SSD_SEQS, SSM_GROUPS, SSM_D_STATE, SSM_D_INNER // SSM_GROUPS), F32),
        ],
        compiler_params=_params(("arbitrary", "arbitrary")),
        name="ssd_mixer",
    )(seqs(xbc_l, seq_len), seqs(xbc_c, ctx_len), seqs(dt_l, seq_len), seqs(dt_c, ctx_len),
      seqs(z_l, seq_len), seqs(z_c, ctx_len), a_log_p, d_skip_x, norm_w, tri, expand)
    return out_l.reshape(n_batch * seq_len, SSM_D_INNER), out_c.reshape(n_batch * ctx_len, SSM_D_INNER)


def _merge_kernel(h_ref, na_ref, ssm_ref, gqa_ref, gate_ref, g_ref, w1_ref, w2_ref, w3_ref, wo_ref, o_ref):
    gates = gate_ref[...]
    t = _sigmoid(gates[:, :D_MODEL].astype(F32)) * _mm(na_ref[...], w1_ref[...])
    t = t + _sigmoid(gates[:, D_MODEL:2 * D_MODEL].astype(F32)) * _mm(ssm_ref[...], w2_ref[...])
    t = t + _sigmoid(gates[:, 2 * D_MODEL:].astype(F32)) * _mm(gqa_ref[...], w3_ref[...])
    o_ref[...] = h_ref[...] + g_ref[...] * _mm(t.astype(BF16), wo_ref[...])


def _merge(h, o_na, o_ssm, o_gqa, gates, mod4, row_fn, w1, w2, w3, wo, tm):
    n_tok = h.shape[0]
    tok = lambda width: pl.BlockSpec((tm, width), lambda i: (i, 0))
    return pl.pallas_call(
        _merge_kernel,
        out_shape=jax.ShapeDtypeStruct((n_tok, D_MODEL), F32),
        grid=(n_tok // tm,),
        in_specs=[tok(D_MODEL), tok(NA_W), tok(SSM_D_INNER), tok(GQA_Q), tok(3 * D_MODEL),
                  _mod_spec(2, row_fn),
                  _resident((NA_W, D_MODEL)), _resident((SSM_D_INNER, D_MODEL)),
                  _resident((GQA_Q, D_MODEL)), _resident((D_MODEL, D_MODEL))],
        out_specs=tok(D_MODEL),
        compiler_params=_params(("arbitrary",)),
        name="branch_merge",
    )(h, o_na, o_ssm, o_gqa, gates, mod4, w1, w2, w3, wo)


def _ffn_kernel(h_ref, hp_ref, hn_ref, sh_ref, sc_ref, g_ref, nw_ref, wup_ref, cw_ref, wdn_ref, fw_ref,
                o_ref, hn_sc, u_sc, act_sc, *, tm, tiles_per_seq, final_norm):
    j = pl.program_id(0) % tiles_per_seq
    seq_first = j == 0
    seq_last = j == tiles_per_seq - 1

    def norm_mod(x):
        return (_rms(x) * nw_ref[...]) * (1.0 + sc_ref[...]) + sh_ref[...]

    x = h_ref[...]
    hn_sc[0:tm] = norm_mod(x).astype(BF16)
    before = jnp.where(seq_first, 0.0, norm_mod(hp_ref[...]))
    after = jnp.where(seq_last, 0.0, norm_mod(hn_ref[...]))
    hn_sc[tm:] = jnp.concatenate([after, before], axis=0).astype(BF16)

    def up(c):
        u_sc[c % 2] = _mm(hn_sc[...], wup_ref[c])

    up(0)
    for c in range(FFN_CHUNKS):
        if c + 1 < FFN_CHUNKS:
            up(c + 1)
        y = _conv3_rows(u_sc[c % 2], tm, cw_ref[c])
        act_sc[:, c * FFN_TN:(c + 1) * FFN_TN] = (_silu(y[:, :FFN_TN]) * y[:, FFN_TN:]).astype(BF16)
    out = x + g_ref[...] * _mm(act_sc[...], wdn_ref[...])
    if final_norm:
        out = _rms(out) * fw_ref[...]
    o_ref[...] = out


def _conv_ffn(h, mod4, row_fn, norm_w, w_up_c, conv_c, w_down_c, final_w, tm, seq_len, final_norm):
    n_tok = h.shape[0]
    return pl.pallas_call(
        functools.partial(_ffn_kernel, tm=tm, tiles_per_seq=seq_len // tm, final_norm=final_norm),
        out_shape=jax.ShapeDtypeStruct((n_tok, D_MODEL), F32),
        grid=(n_tok // tm,),
        in_specs=[
            pl.BlockSpec((tm, D_MODEL), lambda i: (i, 0)),
            *_halo_specs(tm, n_tok),
            _mod_spec(3, row_fn),
            _mod_spec(4, row_fn),
            _mod_spec(5, row_fn),
            _resident((1, D_MODEL)),
            _resident((FFN_CHUNKS, D_MODEL, 2 * FFN_TN)),
            _resident((FFN_CHUNKS, F32_ROWS, 2 * FFN_TN)),
            _resident((FFN_HIDDEN, D_MODEL)),
            _resident((1, D_MODEL)),
        ],
        out_specs=pl.BlockSpec((tm, D_MODEL), lambda i: (i, 0)),
        scratch_shapes=[pltpu.VMEM((tm + 2 * F32_ROWS, D_MODEL), BF16),
                        pltpu.VMEM((2, tm + 2 * F32_ROWS, 2 * FFN_TN), F32),
                        pltpu.VMEM((tm, FFN_HIDDEN), BF16)],
        compiler_params=_params(("arbitrary",)),
        name="conv_ffn",
    )(h, h, h, mod4, mod4, mod4, norm_w, w_up_c, conv_c, w_down_c, final_w)


def _rope_tables(seq_len):
    n_freq = HEAD_DIM // 4
    inv_freq = ROPE_THETA ** (-jnp.arange(n_freq, dtype=F32) / n_freq)
    t = jnp.arange(seq_len, dtype=jnp.int32)
    row = (t // GRID_W).astype(F32)
    col = (t % GRID_W).astype(F32)
    ang = jnp.concatenate([row[:, None] * inv_freq, col[:, None] * inv_freq], axis=-1)
    cos = jnp.repeat(jnp.cos(ang), 2, axis=-1)
    sin = jnp.repeat(jnp.sin(ang), 2, axis=-1) * jnp.tile(jnp.asarray([-1.0, 1.0], F32), HEAD_DIM // 2)
    return jnp.tile(cos, (1, LANES // HEAD_DIM)), jnp.tile(sin, (1, LANES // HEAD_DIM))


def _na_bias_mask(rel_bias):
    qc = np.arange(GRID_W)[:, None]
    kc = np.arange(GRID_W)[None, :]
    win0 = np.clip(qc - NA_WIN_COLS // 2, 0, GRID_W - NA_WIN_COLS)
    ok = (kc >= win0) & (kc < win0 + NA_WIN_COLS)
    dcol = np.clip(kc - qc + NA_WIN_COLS - 1, 0, 2 * NA_WIN_COLS - 2)
    per_drow = jnp.where(jnp.asarray(ok)[None, None], rel_bias.astype(F32)[:, :, dcol], NEG_BIG)
    n_heads = rel_bias.shape[0]
    masked = jnp.full((n_heads, GRID_W, GRID_W), NEG_BIG, F32)
    half = NA_WIN_ROWS // 2
    classes = []
    for cls in range(3):
        rows = []
        for j in range(NA_ROW_GROUP):
            first_slab_row = (0, j, NA_SLAB_ROWS - NA_WIN_ROWS)[cls]
            drow0 = (NA_WIN_ROWS - 1 - j, NA_WIN_ROWS - 1 - half, NA_WIN_ROWS - 1 - half - j)[cls]
            blocks = [per_drow[:, drow0 + sr - first_slab_row] if 0 <= sr - first_slab_row < NA_WIN_ROWS else masked
                      for sr in range(NA_SLAB_ROWS)]
            rows.append(jnp.concatenate(blocks, axis=-1))
        classes.append(jnp.concatenate(rows, axis=1))
    table = jnp.stack(classes, axis=1)
    table = table.reshape(n_heads // 2, 2, 3, NA_ROW_GROUP * GRID_W, NA_SLAB_ROWS * GRID_W)
    return jnp.moveaxis(table, 1, 2).reshape(n_heads // 2, 3, 2 * NA_ROW_GROUP * GRID_W, NA_SLAB_ROWS * GRID_W)


def _pack_w_in(w_in):
    sizes = [3 * NA_W, SSM_D_INNER, SSM_XBC, 2 * SSM_HEADS, GQA_Q, GQA_KV, GQA_KV]
    na, z, xbc, dt, gq, gk, gv, gate = jnp.split(w_in, [int(i) for i in np.cumsum(sizes)], axis=-1)
    na = jnp.concatenate([na[:, :NA_W] * HEAD_DIM ** -0.5, na[:, NA_W:]], axis=-1)
    w_main = jnp.concatenate([na, z, xbc, gq, gk, gv, gate], axis=-1).astype(BF16)
    dt_pad = jnp.zeros((D_MODEL, DT_COLS), F32)
    dt_pad = dt_pad.at[:, :SSM_HEADS].set(dt[:, :SSM_HEADS]).at[:, LANES:LANES + SSM_HEADS].set(dt[:, SSM_HEADS:])
    dt_hi, dt_lo = _split_bf16(dt_pad)
    return w_main, dt_hi, dt_lo


def _pad_heads(v):
    return jnp.zeros((2, 1, LANES), F32).at[:, 0, :SSM_HEADS].set(v.astype(F32))


def _pack_ffn(w_up, conv_w, conv_b, w_down):
    def chunked(t):
        a, b = t[..., :FFN_HIDDEN], t[..., FFN_HIDDEN:]
        lead = t.shape[:-1]
        return jnp.concatenate([a.reshape(lead + (FFN_CHUNKS, FFN_TN)), b.reshape(lead + (FFN_CHUNKS, FFN_TN))], axis=-1)

    w_up_c = jnp.moveaxis(chunked(w_up), 1, 0).astype(BF16)
    taps = jnp.concatenate([conv_w, conv_b[None], jnp.zeros((F32_ROWS - 4, 2 * FFN_HIDDEN), F32)], axis=0)
    conv_c = jnp.moveaxis(chunked(taps), 1, 0)
    return w_up_c, conv_c, w_down.astype(BF16)


def _scan_masks():
    t = np.arange(SSM_CHUNK)
    tril = (t[:, None] >= t[None, :]).astype(np.float32)
    return jnp.asarray(np.stack([tril, tril.T]), BF16)


def kernel(x, c, ctx, c_ctx, w_mod, b_mod, norm1_w, norm2_w, w_in, na_rel_bias, ssm_conv_w, ssm_conv_b, ssm_a_log, ssm_dt_bias, ssm_d, ssm_norm_w, q_norm_w, k_norm_w, w_out_na, w_out_ssm, w_out_gqa, w_o, ffn_w_up, ffn_conv_w, ffn_conv_b, ffn_w_down, final_norm_w):
    n_batch, seq_len, _ = x.shape
    ctx_len = ctx.shape[1]
    depth = w_mod.shape[0]
    assert seq_len % (GRID_W * NA_ROW_GROUP) == 0 and seq_len // GRID_W >= NA_SLAB_ROWS
    assert n_batch <= CTX_MOD_ROW and n_batch % SSD_SEQS == 0 and seq_len % 512 == 0 and ctx_len % 256 == 0
    tm_lat, tm_ctx = 512, 256

    cc = jnp.zeros((MOD_ROWS, D_MODEL), F32).at[:n_batch].set(c).at[CTX_MOD_ROW].set(c_ctx)
    mods = _mod_vectors(cc, w_mod, b_mod)
    lat_row = lambda i: i // (seq_len // tm_lat)
    ctx_row = lambda i: CTX_MOD_ROW

    cos_l, sin_l = _rope_tables(seq_len)
    cos_c, sin_c = jnp.ones((ctx_len, LANES), F32), jnp.zeros((ctx_len, LANES), F32)
    head_of_lane = np.arange(LANES) // HEAD_DIM
    same_head = head_of_lane[:, None] == head_of_lane[None, :]
    seg_ones = jnp.asarray(np.concatenate([same_head, same_head], axis=0), BF16)
    tri = _scan_masks()
    lane_head = np.arange(SSM_D_INNER) // HEAD_DIM
    expand = jnp.asarray(np.arange(LANES)[:, None] == lane_head[None, :], BF16)

    h_lat = x.reshape(n_batch * seq_len, D_MODEL)
    h_ctx = ctx.reshape(n_batch * ctx_len, D_MODEL)
    row = lambda v: v.reshape(1, -1).astype(F32)
    for layer in range(depth):
        need_ctx = layer < depth - 1
        last = layer == depth - 1
        mod4 = mods[layer].reshape(MOD_ROWS, N_MOD, 1, D_MODEL)
        w_main, w_dt_hi, w_dt_lo = _pack_w_in(w_in[layer])
        qw = row(jnp.tile(q_norm_w[layer], LANES // HEAD_DIM))
        kw = row(jnp.tile(k_norm_w[layer], LANES // HEAD_DIM))
        conv_taps = jnp.concatenate(
            [ssm_conv_w[layer], ssm_conv_b[layer][None], jnp.zeros((F32_ROWS - 4, SSM_XBC), F32)], axis=0)
        dt_bias = jnp.zeros((1, DT_COLS), F32)
        dt_bias = dt_bias.at[0, :SSM_HEADS].set(ssm_dt_bias[layer, 0]).at[0, LANES:LANES + SSM_HEADS].set(
            ssm_dt_bias[layer, 1])
        proj = functools.partial(_in_proj, mod4=mod4, norm_w=row(norm1_w[layer]), qw=qw, kw=kw, seg_ones=seg_ones,
                                 conv_taps=conv_taps, dt_bias=dt_bias,
                                 w_main=w_main, w_dt_hi=w_dt_hi, w_dt_lo=w_dt_lo)
        na_l, z_l, xbc_l, dt_l, q_l, k_l, v_l, gate_l = proj(
            h_lat, row_fn=lat_row, cos=cos_l, sin=sin_l, tm=tm_lat, seq_len=seq_len)
        na_c, z_c, xbc_c, dt_c, q_c, k_c, v_c, gate_c = proj(
            h_ctx, row_fn=ctx_row, cos=cos_c, sin=sin_c, tm=tm_ctx, seq_len=ctx_len)

        o_na_l = _neighbourhood_attention(na_l, na_c, _na_bias_mask(na_rel_bias[layer]), n_batch, seq_len, ctx_len)
        o_ssm_l, o_ssm_c = _ssd_mixer(
            xbc_l, dt_l, z_l, xbc_c, dt_c, z_c, _pad_heads(ssm_a_log[layer]),
            row(jnp.repeat(ssm_d[layer], HEAD_DIM)), row(ssm_norm_w[layer]), tri, expand,
            n_batch, seq_len, ctx_len)
        per_batch = lambda t, n: t.reshape(GQA_KV_HEADS, n_batch, n, HEAD_DIM)
        with_ones = lambda v: jnp.concatenate(
            [v, jnp.ones(v.shape[:-1] + (1,), BF16), jnp.zeros(v.shape[:-1] + (LANES - HEAD_DIM - 1,), BF16)], axis=-1)
        k_ctx, v_ctx = per_batch(k_c, ctx_len), with_ones(per_batch(v_c, ctx_len))
        k_all = jnp.concatenate([k_ctx, per_batch(k_l, seq_len)], axis=2)
        v_all = jnp.concatenate([v_ctx, with_ones(per_batch(v_l, seq_len))], axis=2)
        o_gqa_l = _gqa_attention(q_l, k_all, v_all, n_batch, seq_len, tq=256)

        w1, w2, w3, wo = (w.astype(BF16) for w in (w_out_na[layer], w_out_ssm[layer], w_out_gqa[layer], w_o[layer]))
        ffn_w = _pack_ffn(ffn_w_up[layer], ffn_conv_w[layer], ffn_conv_b[layer], ffn_w_down[layer])
        h_lat = _merge(h_lat, o_na_l, o_ssm_l, o_gqa_l, gate_l, mod4, lat_row, w1, w2, w3, wo, tm_lat)
        h_lat = _conv_ffn(h_lat, mod4, lat_row, row(norm2_w[layer]), *ffn_w, row(final_norm_w),
                          tm=tm_lat, seq_len=seq_len, final_norm=last)
        if need_ctx:
            o_na_c = _context_mha(na_c, n_batch, ctx_len)
            o_gqa_c = _gqa_attention(q_c, k_ctx, v_ctx, n_batch, ctx_len, tq=ctx_len)
            h_ctx = _merge(h_ctx, o_na_c, o_ssm_c, o_gqa_c, gate_c, mod4, ctx_row, w1, w2, w3, wo, tm_ctx)
            h_ctx = _conv_ffn(h_ctx, mod4, ctx_row, row(norm2_w[layer]), *ffn_w, row(final_norm_w),
                              tm=tm_ctx, seq_len=ctx_len, final_norm=False)
    return h_lat.reshape(n_batch, seq_len, D_MODEL)
```

```python
import functools

import numpy as np
import jax
import jax.numpy as jnp
from jax import lax
from jax.experimental import pallas as pl
from jax.experimental.pallas import tpu as pltpu

F32 = jnp.float32
BF16 = jnp.bfloat16

D_MODEL = 1024
GRID_W = 64
EPS = 1e-6
HEAD_DIM = 64
NA_HEADS = 8
NA_W = NA_HEADS * HEAD_DIM
NA_WIN_ROWS = 8
NA_WIN_COLS = 16
NA_ROW_GROUP = 4
NA_SLAB_ROWS = 12
SSM_HEADS = 16
SSM_D_INNER = SSM_HEADS * HEAD_DIM
SSM_GROUPS = 2
SSM_D_STATE = 64
SSM_BC = SSM_GROUPS * SSM_D_STATE
SSM_XBC = SSM_D_INNER + 2 * SSM_BC
SSM_CHUNK = 128
SSD_SEQS = 2
GQA_HEADS = 8
GQA_KV_HEADS = 2
GQA_GROUP = GQA_HEADS // GQA_KV_HEADS
GQA_Q = GQA_HEADS * HEAD_DIM
GQA_KV = GQA_KV_HEADS * HEAD_DIM
GQA_BK = 256
GQA_ROW_CHUNK = 64
ROPE_THETA = 10000.0
FFN_HIDDEN = 2816
N_MOD = 6
MOD_ROWS = 16
CTX_MOD_ROW = 8
NEG_BIG = -1e30

LANES = 128
BF16_ROWS = 16
F32_ROWS = 8
VMEM_LIMIT = 56 * 1024 * 1024

SEG_NA = (0, 3 * NA_W)
SEG_Z = (SEG_NA[1], SEG_NA[1] + SSM_D_INNER)
SEG_XBC = (SEG_Z[1], SEG_Z[1] + SSM_XBC)
SEG_GQ = (SEG_XBC[1], SEG_XBC[1] + GQA_Q)
SEG_GK = (SEG_GQ[1], SEG_GQ[1] + GQA_KV)
SEG_GV = (SEG_GK[1], SEG_GK[1] + GQA_KV)
SEG_GATE = (SEG_GV[1], SEG_GV[1] + 3 * D_MODEL)
W_MAIN_COLS = SEG_GATE[1]
DT_COLS = 2 * LANES
DOT_COLS = 512

FFN_TN = 256
FFN_CHUNKS = FFN_HIDDEN // FFN_TN


def _mm(a, b):
    return jnp.dot(a, b, preferred_element_type=F32)


def _mm_nt(a, b):
    return lax.dot_general(a, b, (((1,), (1,)), ((), ())), preferred_element_type=F32)


def _split_bf16(x):
    hi = x.astype(BF16)
    lo = (x - hi.astype(F32)).astype(BF16)
    return hi, lo


def _sigmoid(x):
    return 1.0 / (1.0 + jnp.exp(-x))


def _silu(x):
    return x * _sigmoid(x)


def _softplus(x):
    return jnp.maximum(x, 0.0) + jnp.log1p(jnp.exp(-jnp.abs(x)))


def _rms(x):
    return x * lax.rsqrt(jnp.mean(x * x, axis=-1, keepdims=True) + EPS)


def _resident(shape):
    nd = len(shape)
    return pl.BlockSpec(shape, lambda *_: (0,) * nd, pipeline_mode=pl.Buffered(1))


def _params(sem):
    return pltpu.CompilerParams(dimension_semantics=sem, vmem_limit_bytes=VMEM_LIMIT)


def _mod_kernel(c_ref, w_ref, b_ref, o_ref):
    x_hi, x_lo = _split_bf16(_silu(c_ref[...]))
    w_hi, w_lo = _split_bf16(w_ref[...])
    o_ref[...] = _mm(x_hi, w_hi) + _mm(x_lo, w_hi) + _mm(x_hi, w_lo) + b_ref[...]


def _mod_vectors(cc, w_mod, b_mod):
    n_layers = w_mod.shape[0]
    tn = D_MODEL
    return pl.pallas_call(
        _mod_kernel,
        out_shape=jax.ShapeDtypeStruct((n_layers, MOD_ROWS, N_MOD * D_MODEL), F32),
        grid=(n_layers, N_MOD),
        in_specs=[
            pl.BlockSpec((MOD_ROWS, D_MODEL), lambda l, j: (0, 0)),
            pl.BlockSpec((None, D_MODEL, tn), lambda l, j: (l, 0, j)),
            pl.BlockSpec((None, 1, tn), lambda l, j: (l, 0, j)),
        ],
        out_specs=pl.BlockSpec((None, MOD_ROWS, tn), lambda l, j: (l, 0, j)),
        compiler_params=_params(("arbitrary", "arbitrary")),
        name="mod_vectors",
    )(cc, w_mod, b_mod.reshape(n_layers, 1, N_MOD * D_MODEL))


def _mod_spec(which, row_fn):
    return pl.BlockSpec((None, None, 1, D_MODEL), lambda i: (row_fn(i), which, 0, 0))


def _qk_norm_rope(x, w, cos, sin, seg_ones):
    ss = _mm(jnp.concatenate(_split_bf16(x * x), axis=1), seg_ones)
    xn = x * lax.rsqrt(ss * (1.0 / HEAD_DIM) + EPS) * w
    lane = lax.broadcasted_iota(jnp.int32, xn.shape, 1)
    partner = jnp.where((lane & 1) == 0, pltpu.roll(xn, LANES - 1, 1), pltpu.roll(xn, 1, 1))
    return xn * cos + partner * sin


def _conv3_rows(u, tm, taps):
    n = u.shape[0]
    return (taps[0:1] * pltpu.roll(u, 1, 0)[0:tm] + taps[1:2] * u[0:tm]
            + taps[2:3] * pltpu.roll(u, n - 1, 0)[0:tm] + taps[3:4])


def _inproj_kernel(h_ref, hp_ref, hn_ref, sh_ref, sc_ref, nw_ref, cos_ref, sin_ref, qw_ref, kw_ref, seg_ref,
                   cw_ref, dtb_ref, w_ref, wdh_ref, wdl_ref,
                   na_ref, z_ref, xbc_ref, dt_ref, q_ref, k_ref, v_ref, gate_ref, y_sc, *, tm, tiles_per_seq):
    j = pl.program_id(0) % tiles_per_seq

    def norm_mod(x):
        return (_rms(x) * nw_ref[...]) * (1.0 + sc_ref[...]) + sh_ref[...]

    y = norm_mod(h_ref[...])
    y_hi, y_lo = _split_bf16(y)
    before = jnp.where(j == 0, 0.0, norm_mod(hp_ref[...]))
    after = jnp.where(j == tiles_per_seq - 1, 0.0, norm_mod(hn_ref[...]))
    y_sc[0:tm] = y_hi
    y_sc[tm:] = jnp.concatenate([after, before], axis=0).astype(BF16)

    def project(out_ref, seg):
        for c0 in range(seg[0], seg[1], DOT_COLS):
            c1 = min(c0 + DOT_COLS, seg[1])
            out_ref[:, c0 - seg[0]:c1 - seg[0]] = _mm(y_hi, w_ref[:, c0:c1]).astype(out_ref.dtype)

    for c0 in range(SEG_XBC[0], SEG_XBC[1], DOT_COLS):
        c1 = min(c0 + DOT_COLS, SEG_XBC[1])
        cols = slice(c0 - SEG_XBC[0], c1 - SEG_XBC[0])
        xbc_ref[:, cols] = _silu(_conv3_rows(_mm(y_sc[...], w_ref[:, c0:c1]), tm, cw_ref[:, cols])).astype(BF16)
    dt_raw = _mm(y_hi, wdh_ref[...]) + _mm(y_lo, wdh_ref[...]) + _mm(y_hi, wdl_ref[...])
    dt_ref[...] = _softplus(dt_raw + dtb_ref[...])

    cos = cos_ref[...]
    sin = sin_ref[...]
    seg_ones = seg_ref[...]
    for pair in range(GQA_Q // LANES):
        c0 = SEG_GQ[0] + pair * LANES
        xr = _qk_norm_rope(_mm(y_hi, w_ref[:, c0:c0 + LANES]), qw_ref[...], cos, sin, seg_ones)
        xr = (xr * HEAD_DIM ** -0.5).astype(BF16)
        q_ref[2 * pair] = xr[:, :HEAD_DIM]
        q_ref[2 * pair + 1] = xr[:, HEAD_DIM:]
    kr = _qk_norm_rope(_mm(y_hi, w_ref[:, SEG_GK[0]:SEG_GK[1]]), kw_ref[...], cos, sin, seg_ones).astype(BF16)
    k_ref[0] = kr[:, :HEAD_DIM]
    k_ref[1] = kr[:, HEAD_DIM:]
    vv = _mm(y_hi, w_ref[:, SEG_GV[0]:SEG_GV[1]]).astype(BF16)
    v_ref[0] = vv[:, :HEAD_DIM]
    v_ref[1] = vv[:, HEAD_DIM:]
    project(na_ref, SEG_NA)
    project(z_ref, SEG_Z)
    project(gate_ref, SEG_GATE)


def _halo_specs(tm, n_tok):
    per_tile = tm // F32_ROWS
    last = n_tok // F32_ROWS - 1
    return [pl.BlockSpec((F32_ROWS, D_MODEL), lambda i: (jnp.maximum(i * per_tile - 1, 0), 0)),
            pl.BlockSpec((F32_ROWS, D_MODEL), lambda i: (jnp.minimum((i + 1) * per_tile, last), 0))]


def _in_proj(h, mod4, row_fn, norm_w, cos, sin, qw, kw, seg_ones, conv_taps, dt_bias, w_main, w_dt_hi, w_dt_lo,
             tm, seq_len):
    n_tok = h.shape[0]
    pos_tiles = seq_len // tm
    tok = lambda width: pl.BlockSpec((tm, width), lambda i: (i, 0))
    hm = lambda nh: pl.BlockSpec((nh, tm, HEAD_DIM), lambda i: (0, i, 0))
    outs = pl.pallas_call(
        functools.partial(_inproj_kernel, tm=tm, tiles_per_seq=pos_tiles),
        out_shape=(
            jax.ShapeDtypeStruct((n_tok, 3 * NA_W), BF16),
            jax.ShapeDtypeStruct((n_tok, SSM_D_INNER), BF16),
            jax.ShapeDtypeStruct((n_tok, SSM_XBC), BF16),
            jax.ShapeDtypeStruct((n_tok, DT_COLS), F32),
            jax.ShapeDtypeStruct((GQA_HEADS, n_tok, HEAD_DIM), BF16),
            jax.ShapeDtypeStruct((GQA_KV_HEADS, n_tok, HEAD_DIM), BF16),
            jax.ShapeDtypeStruct((GQA_KV_HEADS, n_tok, HEAD_DIM), BF16),
            jax.ShapeDtypeStruct((n_tok, 3 * D_MODEL), BF16),
        ),
        grid=(n_tok // tm,),
        in_specs=[
            tok(D_MODEL),
            *_halo_specs(tm, n_tok),
            _mod_spec(0, row_fn),
            _mod_spec(1, row_fn),
            _resident((1, D_MODEL)),
            pl.BlockSpec((tm, LANES), lambda i: (i % pos_tiles, 0)),
            pl.BlockSpec((tm, LANES), lambda i: (i % pos_tiles, 0)),
            _resident((1, LANES)),
            _resident((1, LANES)),
            _resident((2 * LANES, LANES)),
            _resident((F32_ROWS, SSM_XBC)),
            _resident((1, DT_COLS)),
            _resident((D_MODEL, W_MAIN_COLS)),
            _resident((D_MODEL, DT_COLS)),
            _resident((D_MODEL, DT_COLS)),
        ],
        out_specs=(tok(3 * NA_W), tok(SSM_D_INNER), tok(SSM_XBC), tok(DT_COLS),
                   hm(GQA_HEADS), hm(GQA_KV_HEADS), hm(GQA_KV_HEADS), tok(3 * D_MODEL)),
        scratch_shapes=[pltpu.VMEM((tm + 2 * F32_ROWS, D_MODEL), BF16)],
        compiler_params=_params(("arbitrary",)),
        name="in_proj",
    )(h, h, h, mod4, mod4, norm_w, cos, sin, qw, kw, seg_ones, conv_taps, dt_bias, w_main, w_dt_hi, w_dt_lo)
    return outs


def _softmax_pv(s_list, v_list):
    m = s_list[0].max(axis=-1, keepdims=True)
    for s in s_list[1:]:
        m = jnp.maximum(m, s.max(axis=-1, keepdims=True))
    den = 0.0
    out = 0.0
    for s, v in zip(s_list, v_list):
        p = jnp.exp(s - m)
        den = den + p.sum(axis=-1, keepdims=True)
        out = out + _mm(p.astype(BF16), v)
    return out / den


def _na_kernel(q_ref, k_ref, v_ref, kc_ref, vc_ref, bm_ref, o_ref, *, n_rows):
    n_groups = n_rows // NA_ROW_GROUP
    gq = NA_ROW_GROUP * GRID_W
    slab = NA_SLAB_ROWS * GRID_W
    kc = kc_ref[...]
    vc = vc_ref[...]
    head0 = lax.broadcasted_iota(jnp.int32, (1, LANES), 1) < HEAD_DIM

    def group(i, carry):
        r0 = i * NA_ROW_GROUP
        start = jnp.clip(r0 - NA_WIN_ROWS // 2, 0, n_rows - NA_SLAB_ROWS)
        cls = jnp.where(i == 0, 0, jnp.where(i == n_groups - 1, 2, 1))
        q2 = q_ref[pl.ds(pl.multiple_of(r0 * GRID_W, gq), gq), :]
        ks = k_ref[pl.ds(pl.multiple_of(start * GRID_W, GRID_W), slab), :]
        vs = v_ref[pl.ds(pl.multiple_of(start * GRID_W, GRID_W), slab), :]
        zero = jnp.zeros_like(q2)
        qs = jnp.concatenate([jnp.where(head0, q2, zero), jnp.where(head0, zero, q2)], axis=0)
        s_win = _mm_nt(qs, ks) + bm_ref[cls]
        s_ctx = _mm_nt(qs, kc)
        o = _softmax_pv([s_win, s_ctx], [vs, vc])
        o_ref[pl.ds(pl.multiple_of(r0 * GRID_W, gq), gq), :] = jnp.where(head0, o[:gq], o[gq:]).astype(BF16)
        return carry

    lax.fori_loop(0, n_groups, group, 0, unroll=8)


def _neighbourhood_attention(na_l, na_c, bias_mask, n_batch, seq_len, ctx_len):
    n_rows = seq_len // GRID_W
    n_pairs = NA_W // LANES
    lat = lambda part: pl.BlockSpec((seq_len, LANES), lambda b, p: (b, part * n_pairs + p))
    ctx = lambda part: pl.BlockSpec((ctx_len, LANES), lambda b, p: (b, part * n_pairs + p))
    return pl.pallas_call(
        functools.partial(_na_kernel, n_rows=n_rows),
        out_shape=jax.ShapeDtypeStruct((n_batch * seq_len, NA_W), BF16),
        grid=(n_batch, n_pairs),
        in_specs=[lat(0), lat(1), lat(2), ctx(1), ctx(2),
                  pl.BlockSpec((None, 3, 2 * NA_ROW_GROUP * GRID_W, NA_SLAB_ROWS * GRID_W),
                               lambda b, p: (p, 0, 0, 0))],
        out_specs=pl.BlockSpec((seq_len, LANES), lambda b, p: (b, p)),
        compiler_params=_params(("arbitrary", "arbitrary")),
        name="neighbourhood_attention",
    )(na_l, na_l, na_l, na_c, na_c, bias_mask)


def _ctx_mha_kernel(q_ref, k_ref, v_ref, o_ref):
    q2 = q_ref[...]
    k2 = k_ref[...]
    v2 = v_ref[...]
    head0 = lax.broadcasted_iota(jnp.int32, (1, LANES), 1) < HEAD_DIM
    outs = []
    for hh in range(2):
        keep = head0 if hh == 0 else jnp.logical_not(head0)
        qm = jnp.where(keep, q2, jnp.zeros_like(q2))
        outs.append(_softmax_pv([_mm_nt(qm, k2)], [v2]))
    o_ref[...] = jnp.where(head0, outs[0], outs[1]).astype(BF16)


def _context_mha(na_c, n_batch, ctx_len):
    n_pairs = NA_W // LANES
    part = lambda which: pl.BlockSpec((ctx_len, LANES), lambda b, p: (b, which * n_pairs + p))
    return pl.pallas_call(
        _ctx_mha_kernel,
        out_shape=jax.ShapeDtypeStruct((n_batch * ctx_len, NA_W), BF16),
        grid=(n_batch, n_pairs),
        in_specs=[part(0), part(1), part(2)],
        out_specs=pl.BlockSpec((ctx_len, LANES), lambda b, p: (b, p)),
        compiler_params=_params(("arbitrary", "arbitrary")),
        name="context_mha",
    )(na_c, na_c, na_c)


def _gqa_kernel(q_ref, k_ref, v_ref, o_ref, s_sc, p_sc, m_sc, a_sc, acc_sc, *, tq, blocks):
    rows = GQA_GROUP * tq
    q = q_ref[...].reshape(rows, HEAD_DIM)
    m_sc[...] = jnp.full_like(m_sc, NEG_BIG)
    acc_sc[...] = jnp.zeros_like(acc_sc)

    def scores(blk, slot):
        k0, size = blk
        s_sc[slot, :, 0:size] = _mm_nt(q, k_ref[k0:k0 + size, :])

    def absorb(blk, slot):
        k0, size = blk
        for c0 in range(0, rows, GQA_ROW_CHUNK):
            rs = slice(c0, c0 + GQA_ROW_CHUNK)
            tiles = [s_sc[slot, rs, t * LANES:(t + 1) * LANES] for t in range(size // LANES)]
            m_old = m_sc[rs, :]
            m_new = jnp.maximum(m_old, functools.reduce(jnp.maximum, tiles).max(axis=-1, keepdims=True))
            a_sc[rs, :] = jnp.exp(m_old - m_new)
            m_sc[rs, :] = m_new
            for t, s in enumerate(tiles):
                p_sc[rs, t * LANES:(t + 1) * LANES] = jnp.exp(s - m_new).astype(BF16)
        acc_sc[...] = a_sc[...] * acc_sc[...] + _mm(p_sc[:, 0:size], v_ref[k0:k0 + size, :])

    scores(blocks[0], 0)
    for j, blk in enumerate(blocks):
        if j + 1 < len(blocks):
            scores(blocks[j + 1], (j + 1) % 2)
        absorb(blk, j % 2)
    acc = acc_sc[...]
    o = acc[:, :HEAD_DIM] / acc[:, HEAD_DIM:HEAD_DIM + 1]
    o_ref[...] = jnp.concatenate([o[g * tq:(g + 1) * tq] for g in range(GQA_GROUP)], axis=-1).astype(BF16)


def _gqa_attention(q_hm, k_all, v_aug, n_batch, q_len, tq):
    q_tiles = q_len // tq
    nk = k_all.shape[2]
    assert nk % (2 * LANES) == 0
    first = nk % GQA_BK
    blocks = ([(0, first)] if first else []) + [(k0, GQA_BK) for k0 in range(first, nk, GQA_BK)]
    rows = GQA_GROUP * tq
    return pl.pallas_call(
        functools.partial(_gqa_kernel, tq=tq, blocks=tuple(blocks)),
        out_shape=jax.ShapeDtypeStruct((n_batch * q_len, GQA_Q), BF16),
        grid=(n_batch, GQA_KV_HEADS, q_tiles),
        in_specs=[
            pl.BlockSpec((GQA_GROUP, tq, HEAD_DIM), lambda b, g, i: (g, b * q_tiles + i, 0)),
            pl.BlockSpec((None, None, nk, HEAD_DIM), lambda b, g, i: (g, b, 0, 0)),
            pl.BlockSpec((None, None, nk, LANES), lambda b, g, i: (g, b, 0, 0)),
        ],
        out_specs=pl.BlockSpec((tq, GQA_GROUP * HEAD_DIM), lambda b, g, i: (b * q_tiles + i, g)),
        scratch_shapes=[
            pltpu.VMEM((2, rows, GQA_BK), F32),
            pltpu.VMEM((rows, GQA_BK), BF16),
            pltpu.VMEM((rows, LANES), F32),
            pltpu.VMEM((rows, LANES), F32),
            pltpu.VMEM((rows, LANES), F32),
        ],
        compiler_params=_params(("arbitrary", "arbitrary", "arbitrary")),
        name="gqa_attention",
    )(q_hm, k_all, v_aug)


def _ssd_kernel(xl_ref, xc_ref, dtl_ref, dtc_ref, zl_ref, zc_ref, alog_ref, dsk_ref, nw_ref, tri_ref, exp_ref,
                outl_ref, outc_ref, x_sc, dt_sc, z_sc, y_sc, st_sc, *, n_chunks, n_ctx_chunks):
    s = pl.program_id(1)
    direction = s // n_chunks
    sp = s % n_chunks
    gid = jnp.where(direction == 0, sp,
                    jnp.where(sp < n_ctx_chunks, n_ctx_chunks - 1 - sp, n_chunks + n_ctx_chunks - 1 - sp))
    is_ctx = gid < n_ctx_chunks

    def stage(x_ref, dt_ref, z_ref):
        x_sc[...] = x_ref[...]
        dt_sc[...] = dt_ref[...]
        z_sc[...] = z_ref[...]

    @pl.when(is_ctx)
    def _():
        stage(xc_ref, dtc_ref, zc_ref)

    @pl.when(jnp.logical_not(is_ctx))
    def _():
        stage(xl_ref, dtl_ref, zl_ref)

    @pl.when(sp == 0)
    def _():
        st_sc[...] = jnp.zeros_like(st_sc)

    tri = tri_ref[direction]
    causal = tri > 0
    neg_a = -jnp.exp(alog_ref[...])
    expand = exp_ref[...]
    heads_per_group = SSM_HEADS // SSM_GROUPS
    group_lanes = heads_per_group * HEAD_DIM

    def chunk_scan(e):
        xs = x_sc[e, :, :SSM_D_INNER]
        dt = dt_sc[e]
        da = dt * neg_a
        da_hi, da_lo = _split_bf16(da)
        cs = _mm(tri, da_hi) + _mm(tri, da_lo)
        tot = da.sum(axis=0, keepdims=True)
        w = dt * jnp.exp(tot - cs)
        e_cs = jnp.exp(cs)
        cs_t = cs.T
        dt_t = dt.T
        w_hi, w_lo = _split_bf16(w)
        w_x = _mm(w_hi, expand) + _mm(w_lo, expand)
        e_hi, e_lo = _split_bf16(e_cs)
        e_cs_x = _mm(e_hi, expand) + _mm(e_lo, expand)
        t_hi, t_lo = _split_bf16(jnp.broadcast_to(jnp.exp(tot), (F32_ROWS, LANES)))
        e_tot_x = (_mm(t_hi, expand) + _mm(t_lo, expand))[0:1]
        xw = (xs.astype(F32) * w_x).astype(BF16)
        y_groups = []
        for g in range(SSM_GROUPS):
            b_g = x_sc[e, :, SSM_D_INNER + g * SSM_D_STATE:SSM_D_INNER + (g + 1) * SSM_D_STATE]
            c_g = x_sc[e, :, SSM_D_INNER + SSM_BC + g * SSM_D_STATE:SSM_D_INNER + SSM_BC + (g + 1) * SSM_D_STATE]
            cb = _mm_nt(c_g, b_g)
            lanes = slice(g * group_lanes, (g + 1) * group_lanes)
            st = st_sc[e, g]
            y_off = _mm(c_g, st.astype(BF16)) * e_cs_x[:, lanes]
            st_sc[e, g] = e_tot_x[:, lanes] * st + _mm(b_g.astype(F32).T.astype(BF16), xw[:, lanes])
            ys = []
            for h in range(g * heads_per_group, (g + 1) * heads_per_group):
                decay = jnp.exp(jnp.where(causal, cs[:, h:h + 1] - cs_t[h:h + 1, :], NEG_BIG))
                ys.append(_mm((cb * decay * dt_t[h:h + 1, :]).astype(BF16), xs[:, h * HEAD_DIM:(h + 1) * HEAD_DIM]))
            y_groups.append(jnp.concatenate(ys, axis=-1) + y_off)
        return jnp.concatenate(y_groups, axis=-1)

    ys = [chunk_scan(e) for e in range(SSD_SEQS)]

    @pl.when(direction == 0)
    def _():
        for e in range(SSD_SEQS):
            y_sc[e, gid] = ys[e] + dsk_ref[...] * x_sc[e, :, :SSM_D_INNER].astype(F32)

    @pl.when(direction == 1)
    def _():
        res = []
        for e in range(SSD_SEQS):
            gated = (y_sc[e, gid] + ys[e]) * _silu(z_sc[e].astype(F32))
            res.append((_rms(gated) * nw_ref[...]).astype(BF16))

        @pl.when(is_ctx)
        def _():
            for e in range(SSD_SEQS):
                outc_ref[e] = res[e]

        @pl.when(jnp.logical_not(is_ctx))
        def _():
            for e in range(SSD_SEQS):
                outl_ref[e] = res[e]


def _ssd_mixer(xbc_l, dt_l, z_l, xbc_c, dt_c, z_c, a_log_p, d_skip_x, norm_w, tri, expand,
               n_batch, seq_len, ctx_len):
    T = SSM_CHUNK
    nlc = seq_len // T
    ncc = ctx_len // T
    n_chunks = nlc + ncc

    def ids(s):
        direction = s // n_chunks
        sp = s % n_chunks
        gid = jnp.where(direction == 0, sp, jnp.where(sp < ncc, ncc - 1 - sp, n_chunks + ncc - 1 - sp))
        return direction, gid, jnp.clip(gid, 0, ncc - 1), jnp.clip(gid - ncc, 0, nlc - 1)

    def lat_main(b, s):
        return (b, ids(s)[3], 0)

    def ctx_main(b, s):
        return (b, ids(s)[2], 0)

    def lat_dt(b, s):
        return (b, ids(s)[3], ids(s)[0])

    def ctx_dt(b, s):
        return (b, ids(s)[2], ids(s)[0])

    def lat_out(b, s):
        direction, gid, _, lc = ids(s)
        return (b, jnp.where(jnp.logical_and(direction == 1, gid >= ncc), lc, nlc - 1), 0)

    def ctx_out(b, s):
        direction, gid, cc, _ = ids(s)
        return (b, jnp.where(direction == 1, jnp.where(gid < ncc, cc, 0), ncc - 1), 0)

    def by_dir(b, s):
        return (ids(s)[0], 0, 0)

    const2 = lambda b, s: (0, 0)
    const3 = lambda b, s: (0, 0, 0)
    seqs = lambda t, n: t.reshape(n_batch, n, t.shape[-1])
    blk = lambda width, index_map: pl.BlockSpec((SSD_SEQS, T, width), index_map)
    out_l, out_c = pl.pallas_call(
        functools.partial(_ssd_kernel, n_chunks=n_chunks, n_ctx_chunks=ncc),
        out_shape=(jax.ShapeDtypeStruct((n_batch, seq_len, SSM_D_INNER), BF16),
                   jax.ShapeDtypeStruct((n_batch, ctx_len, SSM_D_INNER), BF16)),
        grid=(n_batch // SSD_SEQS, 2 * n_chunks),
        in_specs=[
            blk(SSM_XBC, lat_main),
            blk(SSM_XBC, ctx_main),
            blk(LANES, lat_dt),
            blk(LANES, ctx_dt),
            blk(SSM_D_INNER, lat_out),
            blk(SSM_D_INNER, ctx_out),
            pl.BlockSpec((None, 1, LANES), by_dir),
            pl.BlockSpec((1, SSM_D_INNER), const2),
            pl.BlockSpec((1, SSM_D_INNER), const2),
            pl.BlockSpec((2, T, T), const3),
            pl.BlockSpec((LANES, SSM_D_INNER), const2),
        ],
        out_specs=(blk(SSM_D_INNER, lat_out), blk(SSM_D_INNER, ctx_out)),
        scratch_shapes=[
            pltpu.VMEM((SSD_SEQS, T, SSM_XBC), BF16),
            pltpu.VMEM((SSD_SEQS, T, LANES), F32),
            pltpu.VMEM((SSD_SEQS, T, SSM_D_INNER), BF16),
            pltpu.VMEM((SSD_SEQS, n_chunks, T, SSM_D_INNER), F32),
            pltpu.VMEM((SSD_SEQS, SSM_GROUPS, SSM_D_STATE, SSM_D_INNER // SSM_GROUPS), F32),
        ],
        compiler_params=_params(("arbitrary", "arbitrary")),
        name="ssd_mixer",
    )(seqs(xbc_l, seq_len), seqs(xbc_c, ctx_len), seqs(dt_l, seq_len), seqs(dt_c, ctx_len),
      seqs(z_l, seq_len), seqs(z_c, ctx_len), a_log_p, d_skip_x, norm_w, tri, expand)
    return out_l.reshape(n_batch * seq_len, SSM_D_INNER), out_c.reshape(n_batch * ctx_len, SSM_D_INNER)


def _merge_kernel(h_ref, na_ref, ssm_ref, gqa_ref, gate_ref, g_ref, w1_ref, w2_ref, w3_ref, wo_ref, o_ref):
    gates = gate_ref[...]
    t = _sigmoid(gates[:, :D_MODEL].astype(F32)) * _mm(na_ref[...], w1_ref[...])
    t = t + _sigmoid(gates[:, D_MODEL:2 * D_MODEL].astype(F32)) * _mm(ssm_ref[...], w2_ref[...])
    t = t + _sigmoid(gates[:, 2 * D_MODEL:].astype(F32)) * _mm(gqa_ref[...], w3_ref[...])
    o_ref[...] = h_ref[...] + g_ref[...] * _mm(t.astype(BF16), wo_ref[...])


def _merge(h, o_na, o_ssm, o_gqa, gates, mod4, row_fn, w1, w2, w3, wo, tm):
    n_tok = h.shape[0]
    tok = lambda width: pl.BlockSpec((tm, width), lambda i: (i, 0))
    return pl.pallas_call(
        _merge_kernel,
        out_shape=jax.ShapeDtypeStruct((n_tok, D_MODEL), F32),
        grid=(n_tok // tm,),
        in_specs=[tok(D_MODEL), tok(NA_W), tok(SSM_D_INNER), tok(GQA_Q), tok(3 * D_MODEL),
                  _mod_spec(2, row_fn),
                  _resident((NA_W, D_MODEL)), _resident((SSM_D_INNER, D_MODEL)),
                  _resident((GQA_Q, D_MODEL)), _resident((D_MODEL, D_MODEL))],
        out_specs=tok(D_MODEL),
        compiler_params=_params(("arbitrary",)),
        name="branch_merge",
    )(h, o_na, o_ssm, o_gqa, gates, mod4, w1, w2, w3, wo)


def _ffn_kernel(h_ref, hp_ref, hn_ref, sh_ref, sc_ref, g_ref, nw_ref, wup_ref, cw_ref, wdn_ref, fw_ref,
                o_ref, hn_sc, u_sc, act_sc, *, tm, tiles_per_seq, final_norm):
    j = pl.program_id(0) % tiles_per_seq
    seq_first = j == 0
    seq_last = j == tiles_per_seq - 1

    def norm_mod(x):
        return (_rms(x) * nw_ref[...]) * (1.0 + sc_ref[...]) + sh_ref[...]

    x = h_ref[...]
    hn_sc[0:tm] = norm_mod(x).astype(BF16)
    before = jnp.where(seq_first, 0.0, norm_mod(hp_ref[...]))
    after = jnp.where(seq_last, 0.0, norm_mod(hn_ref[...]))
    hn_sc[tm:] = jnp.concatenate([after, before], axis=0).astype(BF16)

    def up(c):
        u_sc[c % 2] = _mm(hn_sc[...], wup_ref[c])

    up(0)
    for c in range(FFN_CHUNKS):
        if c + 1 < FFN_CHUNKS:
            up(c + 1)
        y = _conv3_rows(u_sc[c % 2], tm, cw_ref[c])
        act_sc[:, c * FFN_TN:(c + 1) * FFN_TN] = (_silu(y[:, :FFN_TN]) * y[:, FFN_TN:]).astype(BF16)
    out = x + g_ref[...] * _mm(act_sc[...], wdn_ref[...])
    if final_norm:
        out = _rms(out) * fw_ref[...]
    o_ref[...] = out


def _conv_ffn(h, mod4, row_fn, norm_w, w_up_c, conv_c, w_down_c, final_w, tm, seq_len, final_norm):
    n_tok = h.shape[0]
    return pl.pallas_call(
        functools.partial(_ffn_kernel, tm=tm, tiles_per_seq=seq_len // tm, final_norm=final_norm),
        out_shape=jax.ShapeDtypeStruct((n_tok, D_MODEL), F32),
        grid=(n_tok // tm,),
        in_specs=[
            pl.BlockSpec((tm, D_MODEL), lambda i: (i, 0)),
            *_halo_specs(tm, n_tok),
            _mod_spec(3, row_fn),
            _mod_spec(4, row_fn),
            _mod_spec(5, row_fn),
            _resident((1, D_MODEL)),
            _resident((FFN_CHUNKS, D_MODEL, 2 * FFN_TN)),
            _resident((FFN_CHUNKS, F32_ROWS, 2 * FFN_TN)),
            _resident((FFN_HIDDEN, D_MODEL)),
            _resident((1, D_MODEL)),
        ],
        out_specs=pl.BlockSpec((tm, D_MODEL), lambda i: (i, 0)),
        scratch_shapes=[pltpu.VMEM((tm + 2 * F32_ROWS, D_MODEL), BF16),
                        pltpu.VMEM((2, tm + 2 * F32_ROWS, 2 * FFN_TN), F32),
                        pltpu.VMEM((tm, FFN_HIDDEN), BF16)],
        compiler_params=_params(("arbitrary",)),
        name="conv_ffn",
    )(h, h, h, mod4, mod4, mod4, norm_w, w_up_c, conv_c, w_down_c, final_w)


def _rope_tables(seq_len):
    n_freq = HEAD_DIM // 4
    inv_freq = ROPE_THETA ** (-jnp.arange(n_freq, dtype=F32) / n_freq)
    t = jnp.arange(seq_len, dtype=jnp.int32)
    row = (t // GRID_W).astype(F32)
    col = (t % GRID_W).astype(F32)
    ang = jnp.concatenate([row[:, None] * inv_freq, col[:, None] * inv_freq], axis=-1)
    cos = jnp.repeat(jnp.cos(ang), 2, axis=-1)
    sin = jnp.repeat(jnp.sin(ang), 2, axis=-1) * jnp.tile(jnp.asarray([-1.0, 1.0], F32), HEAD_DIM // 2)
    return jnp.tile(cos, (1, LANES // HEAD_DIM)), jnp.tile(sin, (1, LANES // HEAD_DIM))


def _na_bias_mask(rel_bias):
    qc = np.arange(GRID_W)[:, None]
    kc = np.arange(GRID_W)[None, :]
    win0 = np.clip(qc - NA_WIN_COLS // 2, 0, GRID_W - NA_WIN_COLS)
    ok = (kc >= win0) & (kc < win0 + NA_WIN_COLS)
    dcol = np.clip(kc - qc + NA_WIN_COLS - 1, 0, 2 * NA_WIN_COLS - 2)
    per_drow = jnp.where(jnp.asarray(ok)[None, None], rel_bias.astype(F32)[:, :, dcol], NEG_BIG)
    n_heads = rel_bias.shape[0]
    masked = jnp.full((n_heads, GRID_W, GRID_W), NEG_BIG, F32)
    half = NA_WIN_ROWS // 2
    classes = []
    for cls in range(3):
        rows = []
        for j in range(NA_ROW_GROUP):
            first_slab_row = (0, j, NA_SLAB_ROWS - NA_WIN_ROWS)[cls]
            drow0 = (NA_WIN_ROWS - 1 - j, NA_WIN_ROWS - 1 - half, NA_WIN_ROWS - 1 - half - j)[cls]
            blocks = [per_drow[:, drow0 + sr - first_slab_row] if 0 <= sr - first_slab_row < NA_WIN_ROWS else masked
                      for sr in range(NA_SLAB_ROWS)]
            rows.append(jnp.concatenate(blocks, axis=-1))
        classes.append(jnp.concatenate(rows, axis=1))
    table = jnp.stack(classes, axis=1)
    table = table.reshape(n_heads // 2, 2, 3, NA_ROW_GROUP * GRID_W, NA_SLAB_ROWS * GRID_W)
    return jnp.moveaxis(table, 1, 2).reshape(n_heads // 2, 3, 2 * NA_ROW_GROUP * GRID_W, NA_SLAB_ROWS * GRID_W)


def _pack_w_in(w_in):
    sizes = [3 * NA_W, SSM_D_INNER, SSM_XBC, 2 * SSM_HEADS, GQA_Q, GQA_KV, GQA_KV]
    na, z, xbc, dt, gq, gk, gv, gate = jnp.split(w_in, [int(i) for i in np.cumsum(sizes)], axis=-1)
    na = jnp.concatenate([na[:, :NA_W] * HEAD_DIM ** -0.5, na[:, NA_W:]], axis=-1)
    w_main = jnp.concatenate([na, z, xbc, gq, gk, gv, gate], axis=-1).astype(BF16)
    dt_pad = jnp.zeros((D_MODEL, DT_COLS), F32)
    dt_pad = dt_pad.at[:, :SSM_HEADS].set(dt[:, :SSM_HEADS]).at[:, LANES:LANES + SSM_HEADS].set(dt[:, SSM_HEADS:])
    dt_hi, dt_lo = _split_bf16(dt_pad)
    return w_main, dt_hi, dt_lo


def _pad_heads(v):
    return jnp.zeros((2, 1, LANES), F32).at[:, 0, :SSM_HEADS].set(v.astype(F32))


def _pack_ffn(w_up, conv_w, conv_b, w_down):
    def chunked(t):
        a, b = t[..., :FFN_HIDDEN], t[..., FFN_HIDDEN:]
        lead = t.shape[:-1]
        return jnp.concatenate([a.reshape(lead + (FFN_CHUNKS, FFN_TN)), b.reshape(lead + (FFN_CHUNKS, FFN_TN))], axis=-1)

    w_up_c = jnp.moveaxis(chunked(w_up), 1, 0).astype(BF16)
    taps = jnp.concatenate([conv_w, conv_b[None], jnp.zeros((F32_ROWS - 4, 2 * FFN_HIDDEN), F32)], axis=0)
    conv_c = jnp.moveaxis(chunked(taps), 1, 0)
    return w_up_c, conv_c, w_down.astype(BF16)


def _scan_masks():
    t = np.arange(SSM_CHUNK)
    tril = (t[:, None] >= t[None, :]).astype(np.float32)
    return jnp.asarray(np.stack([tril, tril.T]), BF16)


def kernel(x, c, ctx, c_ctx, w_mod, b_mod, norm1_w, norm2_w, w_in, na_rel_bias, ssm_conv_w, ssm_conv_b, ssm_a_log, ssm_dt_bias, ssm_d, ssm_norm_w, q_norm_w, k_norm_w, w_out_na, w_out_ssm, w_out_gqa, w_o, ffn_w_up, ffn_conv_w, ffn_conv_b, ffn_w_down, final_norm_w):
    n_batch, seq_len, _ = x.shape
    ctx_len = ctx.shape[1]
    depth = w_mod.shape[0]
    assert seq_len % (GRID_W * NA_ROW_GROUP) == 0 and seq_len // GRID_W >= NA_SLAB_ROWS
    assert n_batch <= CTX_MOD_ROW and n_batch % SSD_SEQS == 0 and seq_len % 512 == 0 and ctx_len % 256 == 0
    tm_lat, tm_ctx = 512, 256

    cc = jnp.zeros((MOD_ROWS, D_MODEL), F32).at[:n_batch].set(c).at[CTX_MOD_ROW].set(c_ctx)
    mods = _mod_vectors(cc, w_mod, b_mod)
    lat_row = lambda i: i // (seq_len // tm_lat)
    ctx_row = lambda i: CTX_MOD_ROW

    cos_l, sin_l = _rope_tables(seq_len)
    cos_c, sin_c = jnp.ones((ctx_len, LANES), F32), jnp.zeros((ctx_len, LANES), F32)
    head_of_lane = np.arange(LANES) // HEAD_DIM
    same_head = head_of_lane[:, None] == head_of_lane[None, :]
    seg_ones = jnp.asarray(np.concatenate([same_head, same_head], axis=0), BF16)
    tri = _scan_masks()
    lane_head = np.arange(SSM_D_INNER) // HEAD_DIM
    expand = jnp.asarray(np.arange(LANES)[:, None] == lane_head[None, :], BF16)

    h_lat = x.reshape(n_batch * seq_len, D_MODEL)
    h_ctx = ctx.reshape(n_batch * ctx_len, D_MODEL)
    row = lambda v: v.reshape(1, -1).astype(F32)
    for layer in range(depth):
        need_ctx = layer < depth - 1
        last = layer == depth - 1
        mod4 = mods[layer].reshape(MOD_ROWS, N_MOD, 1, D_MODEL)
        w_main, w_dt_hi, w_dt_lo = _pack_w_in(w_in[layer])
        qw = row(jnp.tile(q_norm_w[layer], LANES // HEAD_DIM))
        kw = row(jnp.tile(k_norm_w[layer], LANES // HEAD_DIM))
        conv_taps = jnp.concatenate(
            [ssm_conv_w[layer], ssm_conv_b[layer][None], jnp.zeros((F32_ROWS - 4, SSM_XBC), F32)], axis=0)
        dt_bias = jnp.zeros((1, DT_COLS), F32)
        dt_bias = dt_bias.at[0, :SSM_HEADS].set(ssm_dt_bias[layer, 0]).at[0, LANES:LANES + SSM_HEADS].set(
            ssm_dt_bias[layer, 1])
        proj = functools.partial(_in_proj, mod4=mod4, norm_w=row(norm1_w[layer]), qw=qw, kw=kw, seg_ones=seg_ones,
                                 conv_taps=conv_taps, dt_bias=dt_bias,
                                 w_main=w_main, w_dt_hi=w_dt_hi, w_dt_lo=w_dt_lo)
        na_l, z_l, xbc_l, dt_l, q_l, k_l, v_l, gate_l = proj(
            h_lat, row_fn=lat_row, cos=cos_l, sin=sin_l, tm=tm_lat, seq_len=seq_len)
        na_c, z_c, xbc_c, dt_c, q_c, k_c, v_c, gate_c = proj(
            h_ctx, row_fn=ctx_row, cos=cos_c, sin=sin_c, tm=tm_ctx, seq_len=ctx_len)

        o_na_l = _neighbourhood_attention(na_l, na_c, _na_bias_mask(na_rel_bias[layer]), n_batch, seq_len, ctx_len)
        o_ssm_l, o_ssm_c = _ssd_mixer(
            xbc_l, dt_l, z_l, xbc_c, dt_c, z_c, _pad_heads(ssm_a_log[layer]),
            row(jnp.repeat(ssm_d[layer], HEAD_DIM)), row(ssm_norm_w[layer]), tri, expand,
            n_batch, seq_len, ctx_len)
        per_batch = lambda t, n: t.reshape(GQA_KV_HEADS, n_batch, n, HEAD_DIM)
        with_ones = lambda v: jnp.concatenate(
            [v, jnp.ones(v.shape[:-1] + (1,), BF16), jnp.zeros(v.shape[:-1] + (LANES - HEAD_DIM - 1,), BF16)], axis=-1)
        k_ctx, v_ctx = per_batch(k_c, ctx_len), with_ones(per_batch(v_c, ctx_len))
        k_all = jnp.concatenate([k_ctx, per_batch(k_l, seq_len)], axis=2)
        v_all = jnp.concatenate([v_ctx, with_ones(per_batch(v_l, seq_len))], axis=2)
        o_gqa_l = _gqa_attention(q_l, k_all, v_all, n_batch, seq_len, tq=256)

        w1, w2, w3, wo = (w.astype(BF16) for w in (w_out_na[layer], w_out_ssm[layer], w_out_gqa[layer], w_o[layer]))
        ffn_w = _pack_ffn(ffn_w_up[layer], ffn_conv_w[layer], ffn_conv_b[layer], ffn_w_down[layer])
        h_lat = _merge(h_lat, o_na_l, o_ssm_l, o_gqa_l, gate_l, mod4, lat_row, w1, w2, w3, wo, tm_lat)
        h_lat = _conv_ffn(h_lat, mod4, lat_row, row(norm2_w[layer]), *ffn_w, row(final_norm_w),
                          tm=tm_lat, seq_len=seq_len, final_norm=last)
        if need_ctx:
            o_na_c = _context_mha(na_c, n_batch, ctx_len)
            o_gqa_c = _gqa_attention(q_c, k_ctx, v_ctx, n_batch, ctx_len, tq=ctx_len)
            h_ctx = _merge(h_ctx, o_na_c, o_ssm_c, o_gqa_c, gate_c, mod4, ctx_row, w1, w2, w3, wo, tm_ctx)
            h_ctx = _conv_ffn(h_ctx, mod4, ctx_row, row(norm2_w[layer]), *ffn_w, row(final_norm_w),
                              tm=tm_ctx, seq_len=ctx_len, final_norm=False)
    return h_lat.reshape(n_batch, seq_len, D_MODEL)
```
